```python
import math
import jax
import jax.numpy as jnp
from jax import lax
import numpy as np

D_MODEL = 4096
BATCH = 4
SEQ = 2048
DEPTH = 2
DEC_BATCH = 8
DEC_SEQ = 8
PAST_LEN = 16384
PAGE_SIZE = 128

HEAD_DIM = 128
SB_HEADS = 16
NSA_HEADS = 16
NSA_KV_HEADS = 4
NSA_GROUP = NSA_HEADS // NSA_KV_HEADS
CMP_BLOCK = 32
SEL_BLOCK = 64
SEL_TOPK = 16
WINDOW = 512
DIFF_HEADS = 16
DIFF_HALF = 128
DIFF_VDIM = 2 * DIFF_HALF
PLE_DIM = 256
Q_BLOCK = 128
SEL_Q_BLOCK = 64
N_EVEN = (DEPTH + 1) // 2
N_ODD = DEPTH // 2
EPS = 1e-6
NEG = -1e30
FORCED_SCORE = 1e4

SB_W = SB_HEADS * HEAD_DIM
NSA_W = NSA_HEADS * HEAD_DIM
NSA_KV_W = NSA_KV_HEADS * HEAD_DIM
AB_SPLITS = (SB_W,) * 4 + (NSA_W,) + (NSA_KV_W,) * 6 + (3 * NSA_HEADS, NSA_W)
AB_IN = sum(AB_SPLITS)
AB_OFFSETS = [int(o) for o in np.cumsum(AB_SPLITS)[:-1]]
AB_OUT = SB_W + NSA_W
C_W = DIFF_HEADS * DIFF_VDIM
C_IN = 4 * C_W
C_OFFSETS = [C_W, 2 * C_W, 3 * C_W]

kernel_name = 'hybrid_sb_nsa_diffattn_decode_step'


def rmsnorm(x, g):
    xf = x.astype(jnp.float32)
    y = xf * lax.rsqrt(jnp.mean(xf * xf, axis=-1, keepdims=True) + EPS)
    return (y * g.astype(jnp.float32)).astype(x.dtype)


def alibi_slopes(n):
    return jnp.asarray(2.0 ** (-8.0 * np.arange(1, n + 1) / n), dtype=jnp.float32)


def masked_softmax(s, mask):
    s = jnp.where(mask, s, NEG)
    m = jnp.max(s, axis=-1, keepdims=True)
    e = jnp.where(mask, jnp.exp(s - m), 0.0)
    return e / jnp.maximum(jnp.sum(e, axis=-1, keepdims=True), 1e-30)


def map_query_blocks(fn, block, args):
    n_blk = args[0][0].shape[args[0][1]] // block

    def split(x, ax):
        x = x.reshape(x.shape[:ax] + (n_blk, block) + x.shape[ax + 1:])
        return jnp.moveaxis(x, ax, 0)

    out = lax.map(lambda blk: fn(*blk), tuple(split(x, ax) for x, ax in args))
    out = jnp.moveaxis(out, 0, 1)
    return out.reshape((out.shape[0], n_blk * block) + out.shape[3:])


def stick_breaking(q, qpos, k, v, kpos):
    z = jnp.einsum('bqhd,bkhd->bhqk', q.astype(jnp.float32), k.astype(jnp.float32)) * (q.shape[-1] ** -0.5)
    mask = kpos[None, :] < qpos[:, None]
    log_beta = jnp.where(mask, jax.nn.log_sigmoid(z), 0.0)
    log_keep = jnp.where(mask, jax.nn.log_sigmoid(-z), 0.0)
    later = lax.cumsum(log_keep, axis=3, reverse=True) - log_keep
    a = jnp.where(mask, jnp.exp(log_beta + later), 0.0)
    return jnp.einsum('bhqk,bkhd->bqhd', a, v.astype(jnp.float32))


def compress(x, w):
    B, T, G, D = x.shape
    xb = x.reshape(B, T // CMP_BLOCK, CMP_BLOCK, G, D).astype(jnp.float32)
    return jnp.einsum('bclgd,lg->bcgd', xb, w.astype(jnp.float32))


def nsa_compressed(q, qpos, k_cmp, v_cmp, slopes):
    B, Tq, H, D = q.shape
    n_cmp = k_cmp.shape[1]
    c_end = jnp.arange(n_cmp, dtype=jnp.int32) * CMP_BLOCK + (CMP_BLOCK - 1)
    qg = q.reshape(B, Tq, NSA_KV_HEADS, NSA_GROUP, D).astype(jnp.float32)
    s = jnp.einsum('bqgzd,bcgd->bgzqc', qg, k_cmp) * (D ** -0.5)
    dist = qpos[:, None] - c_end[None, :]
    s = s - slopes.reshape(NSA_KV_HEADS, NSA_GROUP)[:, :, None, None] * dist.astype(jnp.float32)
    p = masked_softmax(s, dist >= 0)
    o = jnp.einsum('bgzqc,bcgd->bqgzd', p, v_cmp).reshape(B, Tq, H, D)
    return o, jnp.sum(p, axis=2)


def nsa_selected(q, qpos, k_slc, v_slc, importance, slopes):
    B, Tq, H, D = q.shape
    n_sel = k_slc.shape[1] // SEL_BLOCK
    score = importance.reshape(B, NSA_KV_HEADS, Tq, n_sel, SEL_BLOCK // CMP_BLOCK).sum(-1)
    blk = jnp.arange(n_sel, dtype=jnp.int32)
    valid = blk[None, :] * SEL_BLOCK <= qpos[:, None]
    forced = (blk[None, :] == (qpos // SEL_BLOCK)[:, None]) | (blk[None, :] == 0)
    score = jnp.where(forced, FORCED_SCORE, jnp.where(valid, score, -1.0))
    _, idx = lax.top_k(score, min(SEL_TOPK, n_sel))
    kb = k_slc.reshape(B, n_sel, SEL_BLOCK, NSA_KV_HEADS, D).transpose(0, 3, 1, 2, 4)
    vb = v_slc.reshape(B, n_sel, SEL_BLOCK, NSA_KV_HEADS, D).transpose(0, 3, 1, 2, 4)
    gather = jax.vmap(jax.vmap(lambda blocks, ids: blocks[ids]))
    g_slopes = slopes.reshape(NSA_KV_HEADS, NSA_GROUP)[None, :, :, None, None]

    def block_fn(qb, pb, ib):
        Qb = qb.shape[1]
        gk = gather(kb, ib).reshape(B, NSA_KV_HEADS, Qb, -1, D).astype(jnp.float32)
        gv = gather(vb, ib).reshape(B, NSA_KV_HEADS, Qb, -1, D).astype(jnp.float32)
        kp = (ib[..., None] * SEL_BLOCK + jnp.arange(SEL_BLOCK, dtype=jnp.int32)).reshape(B, NSA_KV_HEADS, Qb, -1)
        qg = qb.reshape(B, Qb, NSA_KV_HEADS, NSA_GROUP, D).astype(jnp.float32)
        s = jnp.einsum('bqgzd,bgqkd->bgzqk', qg, gk) * (D ** -0.5)
        dist = (pb[None, None, :, None] - kp)[:, :, None]
        s = s - g_slopes * dist.astype(jnp.float32)
        p = masked_softmax(s, dist >= 0)
        return jnp.einsum('bgzqk,bgqkd->bqgzd', p, gv).reshape(B, Qb, H, D)

    return map_query_blocks(block_fn, math.gcd(Tq, SEL_Q_BLOCK), [(q, 1), (qpos, 0), (idx, 2)])


def nsa_window(q, qpos, k_win, v_win, k_start, slopes):
    B, Tq, H, D = q.shape
    pre = k_win.shape[1] - Tq
    k_pad = jnp.pad(k_win, ((0, 0), (WINDOW, 0), (0, 0), (0, 0)))
    v_pad = jnp.pad(v_win, ((0, 0), (WINDOW, 0), (0, 0), (0, 0)))
    kpos = jnp.concatenate([jnp.full((WINDOW,), -1, jnp.int32), k_start + jnp.arange(pre + Tq, dtype=jnp.int32)])
    blk = math.gcd(Tq, Q_BLOCK)
    g_slopes = slopes.reshape(NSA_KV_HEADS, NSA_GROUP)[:, :, None, None]

    def block_fn(qb, pb):
        off = pb[0] - qpos[0] + pre
        kb = lax.dynamic_slice_in_dim(k_pad, off, WINDOW + blk, axis=1).astype(jnp.float32)
        vb = lax.dynamic_slice_in_dim(v_pad, off, WINDOW + blk, axis=1).astype(jnp.float32)
        kp = lax.dynamic_slice_in_dim(kpos, off, WINDOW + blk)
        qg = qb.reshape(B, blk, NSA_KV_HEADS, NSA_GROUP, D).astype(jnp.float32)
        s = jnp.einsum('bqgzd,bkgd->bgzqk', qg, kb) * (D ** -0.5)
        dist = pb[:, None] - kp[None, :]
        mask = (kp[None, :] >= 0) & (dist >= 0) & (dist < WINDOW)
        s = s - g_slopes * dist.astype(jnp.float32)
        p = masked_softmax(s, mask)
        return jnp.einsum('bgzqk,bkgd->bqgzd', p, vb).reshape(B, blk, H, D)

    return map_query_blocks(block_fn, blk, [(q, 1), (qpos, 0)])


def nsa_mix(q, qpos, kv_all, win_all, win_start, branch_logits, cmp_wk, cmp_wv):
    B, Tq, H, D = q.shape
    slopes = alibi_slopes(NSA_HEADS)
    pad = (-kv_all.shape[1]) % SEL_BLOCK
    kv_all = jnp.pad(kv_all, ((0, 0), (0, pad), (0, 0), (0, 0), (0, 0)))
    k_cmp = compress(kv_all[:, :, 0], cmp_wk)
    v_cmp = compress(kv_all[:, :, 1], cmp_wv)
    o_cmp, importance = nsa_compressed(q, qpos, k_cmp, v_cmp, slopes)
    o_slc = nsa_selected(q, qpos, kv_all[:, :, 2], kv_all[:, :, 3], importance, slopes)
    o_win = nsa_window(q, qpos, win_all[:, :, 0], win_all[:, :, 1], win_start, slopes)
    g = jax.nn.sigmoid(branch_logits.astype(jnp.float32)).reshape(B, Tq, H, 3, 1)
    return g[:, :, :, 0] * o_cmp + g[:, :, :, 1] * o_slc + g[:, :, :, 2] * o_win


def layer_ab(hn, pos0, w_in, cmp_wk, cmp_wv, w_out, past_sb, past_nsa, win_buf):
    B, T, _ = hn.shape
    parts = jnp.split(hn @ w_in, AB_OFFSETS, axis=-1)
    sb_q, sb_k, sb_v = (t.reshape(B, T, SB_HEADS, HEAD_DIM) for t in parts[0:3])
    sb_z = parts[3]
    ns_q = parts[4].reshape(B, T, NSA_HEADS, HEAD_DIM)
    kc, vc, ksl, vsl, kw, vw = (t.reshape(B, T, NSA_KV_HEADS, HEAD_DIM) for t in parts[5:11])
    ns_logits, ns_z = parts[11], parts[12]
    qpos = pos0 + jnp.arange(T, dtype=jnp.int32)

    sb_new = jnp.stack([sb_k, sb_v], axis=2)
    nsa_new = jnp.stack([kc, vc, ksl, vsl], axis=2)
    win_new = jnp.stack([kw, vw], axis=2)
    sb_all = sb_new if past_sb is None else jnp.concatenate([past_sb, sb_new], axis=1)
    nsa_all = nsa_new if past_nsa is None else jnp.concatenate([past_nsa, nsa_new], axis=1)
    win_all = win_new if win_buf is None else jnp.concatenate([win_buf, win_new], axis=1)
    n_pre = win_all.shape[1] - T

    k_sb, v_sb = sb_all[:, :, 0], sb_all[:, :, 1]
    kpos = jnp.arange(sb_all.shape[1], dtype=jnp.int32)
    sb_o = map_query_blocks(lambda qb, pb: stick_breaking(qb, pb, k_sb, v_sb, kpos),
                            math.gcd(T, Q_BLOCK), [(sb_q, 1), (qpos, 0)])
    ns_o = nsa_mix(ns_q, qpos, nsa_all, win_all, pos0 - n_pre, ns_logits, cmp_wk, cmp_wv)
    mixed = jnp.concatenate([sb_o.reshape(B, T, SB_W).astype(hn.dtype) * jax.nn.silu(sb_z),
                             ns_o.reshape(B, T, NSA_W).astype(hn.dtype) * jax.nn.silu(ns_z)], axis=-1)
    win_state = win_all[:, win_all.shape[1] - min(WINDOW, win_all.shape[1]):]
    return mixed @ w_out, sb_new, nsa_new, win_state


def diff_attention(q, qpos, k, v, kpos, lam, slopes):
    qf, kf = q.astype(jnp.float32), k.astype(jnp.float32)
    scale = DIFF_HALF ** -0.5
    s1 = jnp.einsum('bqhd,bkhd->bhqk', qf[..., :DIFF_HALF], kf[..., :DIFF_HALF]) * scale
    s2 = jnp.einsum('bqhd,bkhd->bhqk', qf[..., DIFF_HALF:], kf[..., DIFF_HALF:]) * scale
    dist = (qpos[:, None] - kpos[None, :]).astype(jnp.float32)
    bias = -slopes[:, None, None] * dist
    mask = dist >= 0
    a = masked_softmax(s1 + bias, mask) - lam * masked_softmax(s2 + bias, mask)
    return jnp.einsum('bhqk,bkhd->bqhd', a, v.astype(jnp.float32))


def layer_c(hn, pos0, w_in, lq1, lk1, lq2, lk2, head_g, w_out, lambda_init, past_kv):
    B, T, _ = hn.shape
    q, k, v, z = jnp.split(hn @ w_in, C_OFFSETS, axis=-1)
    q = q.reshape(B, T, DIFF_HEADS, 2 * DIFF_HALF)
    k = k.reshape(B, T, DIFF_HEADS, 2 * DIFF_HALF)
    v = v.reshape(B, T, DIFF_HEADS, DIFF_VDIM)
    kv_new = jnp.stack([k, v], axis=2)
    kv_all = kv_new if past_kv is None else jnp.concatenate([past_kv, kv_new], axis=1)
    qpos = pos0 + jnp.arange(T, dtype=jnp.int32)
    kpos = jnp.arange(kv_all.shape[1], dtype=jnp.int32)
    f32 = lambda t: t.astype(jnp.float32)
    lam = jnp.exp(jnp.sum(f32(lq1) * f32(lk1))) - jnp.exp(jnp.sum(f32(lq2) * f32(lk2))) + lambda_init
    slopes = alibi_slopes(DIFF_HEADS)
    k_all, v_all = kv_all[:, :, 0], kv_all[:, :, 1]
    o = map_query_blocks(lambda qb, pb: diff_attention(qb, pb, k_all, v_all, kpos, lam, slopes),
                         math.gcd(T, Q_BLOCK), [(q, 1), (qpos, 0)])
    o = rmsnorm(o, head_g) * (1.0 - lambda_init)
    o = o.reshape(B, T, C_W).astype(hn.dtype) * jax.nn.silu(z)
    return o @ w_out, kv_new


def per_layer_embed(h, p_i, g, w_gate, w_proj):
    gate = jax.nn.sigmoid((rmsnorm(h, g) @ w_gate).astype(jnp.float32))
    return h + (gate * (p_i @ w_proj).astype(jnp.float32)).astype(h.dtype)


def setup_inputs(seed: int = 0) -> dict:
    key = jax.random.key(seed)
    keys = iter(jax.random.split(key, 32))
    nrm = lambda shape, scale: jax.random.normal(next(keys), shape, jnp.float32) * scale
    n_pages = PAST_LEN // PAGE_SIZE
    n_phys = (DEC_BATCH * n_pages * 5) // 4
    wb = min(WINDOW, PAST_LEN)
    page_table = jax.random.permutation(next(keys), n_phys)[:DEC_BATCH * n_pages]
    page_table = page_table.reshape(DEC_BATCH, n_pages).astype(jnp.int32)
    return {
        'x_prompt': nrm((BATCH, SEQ, D_MODEL), 1.0),
        'x_sample': nrm((DEC_BATCH, DEC_SEQ, D_MODEL), 1.0),
        'p_prompt': nrm((DEPTH, BATCH, SEQ, PLE_DIM), 1.0),
        'p_sample': nrm((DEPTH, DEC_BATCH, DEC_SEQ, PLE_DIM), 1.0),
        'cache_sb_kv': nrm((N_EVEN, n_phys, PAGE_SIZE, 2, SB_HEADS, HEAD_DIM), 1.0),
        'cache_nsa_kv': nrm((N_EVEN, n_phys, PAGE_SIZE, 4, NSA_KV_HEADS, HEAD_DIM), 1.0),
        'state_nsa_win_kv': nrm((N_EVEN, DEC_BATCH, wb, 2, NSA_KV_HEADS, HEAD_DIM), 1.0),
        'cache_diff_kv': nrm((N_ODD, n_phys, PAGE_SIZE, 2, DIFF_HEADS, DIFF_VDIM), 1.0),
        'page_table': page_table,
        'norm_g': 1.0 + nrm((DEPTH, D_MODEL), 0.02),
        'w_in_ab': nrm((N_EVEN, D_MODEL, AB_IN), D_MODEL ** -0.5),
        'nsa_cmp_wk': 1.0 / CMP_BLOCK + nrm((N_EVEN, CMP_BLOCK, NSA_KV_HEADS), 0.01),
        'nsa_cmp_wv': 1.0 / CMP_BLOCK + nrm((N_EVEN, CMP_BLOCK, NSA_KV_HEADS), 0.01),
        'w_out_ab': nrm((N_EVEN, AB_OUT, D_MODEL), AB_OUT ** -0.5),
        'w_in_c': nrm((N_ODD, D_MODEL, C_IN), D_MODEL ** -0.5),
        'diff_lq1': nrm((N_ODD, DIFF_HALF), 0.1),
        'diff_lk1': nrm((N_ODD, DIFF_HALF), 0.1),
        'diff_lq2': nrm((N_ODD, DIFF_HALF), 0.1),
        'diff_lk2': nrm((N_ODD, DIFF_HALF), 0.1),
        'diff_head_g': 1.0 + nrm((N_ODD, DIFF_VDIM), 0.02),
        'w_out_c': nrm((N_ODD, C_W, D_MODEL), C_W ** -0.5),
        'ple_norm_g': 1.0 + nrm((DEPTH, D_MODEL), 0.02),
        'w_ple_gate': nrm((DEPTH, D_MODEL, D_MODEL), D_MODEL ** -0.5),
        'w_ple_proj': nrm((DEPTH, PLE_DIM, D_MODEL), PLE_DIM ** -0.5),
        'final_norm_g': 1.0 + nrm((D_MODEL,), 0.02),
    }


def reference(x_prompt, x_sample, p_prompt, p_sample, cache_sb_kv, cache_nsa_kv, state_nsa_win_kv,
              cache_diff_kv, page_table, norm_g, w_in_ab, nsa_cmp_wk, nsa_cmp_wv, w_out_ab, w_in_c,
              diff_lq1, diff_lk1, diff_lq2, diff_lk2, diff_head_g, w_out_c, ple_norm_g, w_ple_gate,
              w_ple_proj, final_norm_g):
    def gather_pages(pool, j):
        g = pool[j, page_table]
        return g.reshape((g.shape[0], g.shape[1] * g.shape[2]) + g.shape[3:])

    def run(x, p, sample):
        pos0 = page_table.shape[1] * PAGE_SIZE if sample else 0
        h = x
        sb_rows, nsa_rows, win_rows, diff_rows = [], [], [], []
        for i in range(DEPTH):
            j = i // 2
            hn = rmsnorm(h, norm_g[i])
            if i % 2 == 0:
                if sample:
                    past = (gather_pages(cache_sb_kv, j), gather_pages(cache_nsa_kv, j), state_nsa_win_kv[j])
                else:
                    past = (None, None, None)
                y, sb_r, nsa_r, win_r = layer_ab(hn, pos0, w_in_ab[j], nsa_cmp_wk[j], nsa_cmp_wv[j],
                                                 w_out_ab[j], past[0], past[1], past[2])
                sb_rows.append(sb_r)
                nsa_rows.append(nsa_r)
                win_rows.append(win_r)
            else:
                past_kv = gather_pages(cache_diff_kv, j) if sample else None
                lambda_init = 0.8 - 0.6 * math.exp(-0.3 * i)
                y, d_r = layer_c(hn, pos0, w_in_c[j], diff_lq1[j], diff_lk1[j], diff_lq2[j], diff_lk2[j],
                                 diff_head_g[j], w_out_c[j], lambda_init, past_kv)
                diff_rows.append(d_r)
            h = h + y
            h = per_layer_embed(h, p[i], ple_norm_g[i], w_ple_gate[i], w_ple_proj[i])
        return (rmsnorm(h, final_norm_g), jnp.stack(sb_rows), jnp.stack(nsa_rows),
                jnp.stack(win_rows), jnp.stack(diff_rows))

    y_prompt, sb_p, nsa_p, win_p, diff_p = run(x_prompt, p_prompt, False)
    y_sample, sb_s, nsa_s, win_s, diff_s = run(x_sample, p_sample, True)
    return (y_prompt, y_sample, sb_p, sb_s, nsa_p, nsa_s, win_p, win_s, diff_p, diff_s)
```

```python
import functools
import math

import jax
import jax.numpy as jnp
import numpy as np
from jax import lax
from jax.experimental import pallas as pl
from jax.experimental.pallas import tpu as pltpu

F32 = jnp.float32
BF16 = jnp.bfloat16

HEAD_DIM = 128
SB_HEADS = 16
NSA_HEADS = 16
NSA_KV_HEADS = 4
NSA_GROUP = NSA_HEADS // NSA_KV_HEADS
CMP_BLOCK = 32
SEL_BLOCK = 64
SEL_TOPK = 16
WINDOW = 512
DIFF_HEADS = 16
DIFF_HALF = 128
DIFF_VDIM = 2 * DIFF_HALF
PAGE_SIZE = 128
EPS = 1e-6
NEG = -1e30
FORCED_SCORE = 1e4

SB_W = SB_HEADS * HEAD_DIM
NSA_W = NSA_HEADS * HEAD_DIM
NSA_KV_W = NSA_KV_HEADS * HEAD_DIM
C_W = DIFF_HEADS * DIFF_VDIM
LANES = 128
VMEM_LIMIT = 56 * 1024 * 1024


def _alibi_slopes(n):
    return np.asarray(2.0 ** (-8.0 * np.arange(1, n + 1) / n), dtype=np.float32)


def _cparams(sem):
    return pltpu.CompilerParams(dimension_semantics=sem, vmem_limit_bytes=VMEM_LIMIT)


def _dot(a, b):
    return jnp.dot(a, b, preferred_element_type=F32)


def _dot_nt(a, b):
    return lax.dot_general(a, b, (((1,), (1,)), ((), ())), preferred_element_type=F32)


def _silu(z):
    return z * (1.0 / (1.0 + jnp.exp(-z)))


def _sigmoid(z):
    return 1.0 / (1.0 + jnp.exp(-z))


def _iota(shape, dim):
    return lax.broadcasted_iota(jnp.int32, shape, dim)


def _rmsnorm_kernel(x_ref, g_ref, o_ref):
    x = x_ref[...]
    ms = jnp.mean(x * x, axis=-1, keepdims=True)
    o_ref[...] = (x * lax.rsqrt(ms + EPS) * g_ref[...]).astype(o_ref.dtype)


def rmsnorm(x, g, out_dtype):
    m, d = x.shape
    tm = min(m, 256)
    return pl.pallas_call(
        _rmsnorm_kernel,
        grid=(m // tm,),
        in_specs=[pl.BlockSpec((tm, d), lambda i: (i, 0)), pl.BlockSpec((1, d), lambda i: (0, 0))],
        out_specs=pl.BlockSpec((tm, d), lambda i: (i, 0)),
        out_shape=jax.ShapeDtypeStruct((m, d), out_dtype),
        compiler_params=_cparams(("parallel",)),
        name="rmsnorm",
    )(x, g.reshape(1, d))


def _mm_kernel(a_ref, w_ref, o_ref):
    o_ref[...] = _dot(a_ref[...], w_ref[...]).astype(o_ref.dtype)


def _mm_res_kernel(a_ref, w_ref, r_ref, o_ref):
    o_ref[...] = r_ref[...] + _dot(a_ref[...], w_ref[...])


def matmul(a, w, residual=None, out_dtype=F32):
    m, k = a.shape
    n = w.shape[1]
    tm = min(m, 1024)
    tn = min(n, 512)
    grid = (m // tm, n // tn)
    in_specs = [pl.BlockSpec((tm, k), lambda i, j: (i, 0)), pl.BlockSpec((k, tn), lambda i, j: (0, j))]
    args = [a, w]
    kern = _mm_kernel
    if residual is not None:
        in_specs.append(pl.BlockSpec((tm, tn), lambda i, j: (i, j)))
        args.append(residual)
        kern = _mm_res_kernel
    return pl.pallas_call(
        kern,
        grid=grid,
        in_specs=in_specs,
        out_specs=pl.BlockSpec((tm, tn), lambda i, j: (i, j)),
        out_shape=jax.ShapeDtypeStruct((m, n), out_dtype),
        compiler_params=_cparams(("parallel", "parallel")),
        name="matmul",
    )(*args)


def _ple_kernel(hn_ref, wg_ref, p_ref, wp_ref, h_ref, o_ref):
    gate = _sigmoid(_dot(hn_ref[...], wg_ref[...]))
    proj = _dot(p_ref[...], wp_ref[...])
    o_ref[...] = h_ref[...] + gate * proj


def ple(hn, wg, p, wp, h):
    m, d = hn.shape
    n = wg.shape[1]
    pd = p.shape[1]
    tm = min(m, 1024)
    tn = min(n, 512)
    return pl.pallas_call(
        _ple_kernel,
        grid=(m // tm, n // tn),
        in_specs=[
            pl.BlockSpec((tm, d), lambda i, j: (i, 0)),
            pl.BlockSpec((d, tn), lambda i, j: (0, j)),
            pl.BlockSpec((tm, pd), lambda i, j: (i, 0)),
            pl.BlockSpec((pd, tn), lambda i, j: (0, j)),
            pl.BlockSpec((tm, tn), lambda i, j: (i, j)),
        ],
        out_specs=pl.BlockSpec((tm, tn), lambda i, j: (i, j)),
        out_shape=jax.ShapeDtypeStruct((m, n), F32),
        compiler_params=_cparams(("parallel", "parallel")),
        name="ple",
    )(hn, wg, p, wp, h)


def _sb_chunk(qb, k, v, mask, carry, acc, scale):
    c = k.shape[0]
    z = _dot_nt(qb, k) * scale
    lb = jnp.minimum(z, 0.0) - jnp.log(1.0 + jnp.exp(-jnp.abs(z)))
    lk = lb - z
    lb = jnp.where(mask, lb, 0.0)
    lk = jnp.where(mask, lk, 0.0)
    incl_mat = (_iota((c, c), 0) >= _iota((c, c), 1)).astype(BF16)
    hi = lk.astype(BF16)
    lo = (lk - hi.astype(F32)).astype(BF16)
    incl = _dot(hi, incl_mat) + _dot(lo, incl_mat)
    later = incl - lk + carry
    a = jnp.where(mask, jnp.exp(lb + later), 0.0)
    acc = acc + _dot(a.astype(BF16), v)
    return carry + incl[:, 0:1], acc


def _sb_prompt_kernel(q_ref, k_ref, v_ref, z_ref, o_ref, *, tq, tk):
    i = pl.program_id(2)
    scale = HEAD_DIM ** -0.5
    qb = q_ref[0].astype(BF16)
    qpos = i * tq + _iota((tq, 1), 0)
    n_chunks = (i + 1) * (tq // tk)

    def body(n, state):
        carry, acc = state
        j = n_chunks - 1 - n
        start = pl.multiple_of(j * tk, tk)
        k = k_ref[0, pl.ds(start, tk), :].astype(BF16)
        v = v_ref[0, pl.ds(start, tk), :].astype(BF16)
        kpos = j * tk + _iota((1, tk), 1)
        return _sb_chunk(qb, k, v, kpos < qpos, carry, acc, scale)

    carry0 = jnp.zeros((tq, 1), F32)
    acc0 = jnp.zeros((tq, HEAD_DIM), F32)
    _, acc = lax.fori_loop(0, n_chunks, body, (carry0, acc0))
    o_ref[0] = (acc * _silu(z_ref[0])).astype(o_ref.dtype)


def sb_prompt(q, kv, z, *, tq=256, tk=128):
    b, t, _ = q.shape
    tq = min(tq, t)
    hq = pl.BlockSpec((1, tq, HEAD_DIM), lambda bi, h, i: (bi, i, h))
    return pl.pallas_call(
        functools.partial(_sb_prompt_kernel, tq=tq, tk=tk),
        grid=(b, SB_HEADS, t // tq),
        in_specs=[
            hq,
            pl.BlockSpec((1, t, HEAD_DIM), lambda bi, h, i: (bi, 0, h)),
            pl.BlockSpec((1, t, HEAD_DIM), lambda bi, h, i: (bi, 0, SB_HEADS + h)),
            hq,
        ],
        out_specs=hq,
        out_shape=jax.ShapeDtypeStruct((b, t, SB_W), BF16),
        compiler_params=_cparams(("parallel", "parallel", "arbitrary")),
        name="sb_prompt",
    )(q, kv, kv, z)


def _block_diag_rows(q, n_heads, width):
    t = q.shape[0]
    tiled = jnp.concatenate([q] * n_heads, axis=0)
    rows = _iota(tiled.shape, 0) // t
    cols = _iota(tiled.shape, 1) // width
    return jnp.where(rows == cols, tiled, 0.0)


def _pad_rows(x, rows):
    return jnp.concatenate([x, jnp.zeros((rows - x.shape[0], x.shape[1]), x.dtype)], axis=0)


def _sb_decode_kernel(pt_ref, q_ref, new_ref, page_ref, z_ref, o_ref, qbd_ref, carry_ref, acc_ref, *, t, n_pages):
    p = pl.program_id(1)
    scale = HEAD_DIM ** -0.5
    rows = SB_HEADS * t
    tq = _iota((rows, 1), 0) % t
    col = _iota((1, PAGE_SIZE), 1)

    def attend(k_all, v_all, mask):
        z = None
        carry = carry_ref[...]
        qbd = qbd_ref[...]
        kb = k_all.astype(BF16)
        zfull = _dot_nt(qbd, kb) * scale
        c = PAGE_SIZE
        lb = jnp.minimum(zfull, 0.0) - jnp.log(1.0 + jnp.exp(-jnp.abs(zfull)))
        lk = lb - zfull
        lb = jnp.where(mask, lb, 0.0)
        lk = jnp.where(mask, lk, 0.0)
        incl_mat = (_iota((c, c), 0) >= _iota((c, c), 1)).astype(BF16)
        hi = lk.astype(BF16)
        lo = (lk - hi.astype(F32)).astype(BF16)
        incl = _dot(hi, incl_mat) + _dot(lo, incl_mat)
        later = incl - lk + carry
        a = jnp.where(mask, jnp.exp(lb + later), 0.0).astype(BF16)
        for h in range(SB_HEADS):
            vh = v_all[:, h * HEAD_DIM:(h + 1) * HEAD_DIM].astype(BF16)
            acc_ref[h * t:(h + 1) * t, :] += _dot(a[h * t:(h + 1) * t, :], vh)
        carry_ref[...] = carry + incl[:, 0:1]
        del z

    @pl.when(p == 0)
    def _():
        qbd_ref[...] = _block_diag_rows(q_ref[0], SB_HEADS, HEAD_DIM).astype(BF16)
        carry_ref[...] = jnp.zeros_like(carry_ref)
        acc_ref[...] = jnp.zeros_like(acc_ref)
        new = _pad_rows(new_ref[0], PAGE_SIZE)
        attend(new[:, :SB_W], new[:, SB_W:], (col < tq))

    @pl.when(p > 0)
    def _():
        attend(page_ref[:, :SB_W], page_ref[:, SB_W:], jnp.full((rows, PAGE_SIZE), True))

    @pl.when(p == n_pages)
    def _():
        zg = z_ref[0]
        for h in range(SB_HEADS):
            sl = slice(h * HEAD_DIM, (h + 1) * HEAD_DIM)
            o_ref[0, :, sl] = (acc_ref[h * t:(h + 1) * t, :] * _silu(zg[:, sl])).astype(o_ref.dtype)


def sb_decode(q, new_kv, cache, page_table, z):
    b, t, _ = q.shape
    n_pages = page_table.shape[1]
    rows = SB_HEADS * t
    tok = lambda w: pl.BlockSpec((1, t, w), lambda bi, p, pt: (bi, 0, 0))
    grid_spec = pltpu.PrefetchScalarGridSpec(
        num_scalar_prefetch=1,
        grid=(b, n_pages + 1),
        in_specs=[
            tok(SB_W),
            tok(2 * SB_W),
            pl.BlockSpec((None, PAGE_SIZE, 2 * SB_W),
                         lambda bi, p, pt: (pt[bi, n_pages - jnp.maximum(p, 1)], 0, 0)),
            tok(SB_W),
        ],
        out_specs=tok(SB_W),
        scratch_shapes=[
            pltpu.VMEM((rows, SB_W), BF16),
            pltpu.VMEM((rows, 1), F32),
            pltpu.VMEM((rows, HEAD_DIM), F32),
        ],
    )
    return pl.pallas_call(
        functools.partial(_sb_decode_kernel, t=t, n_pages=n_pages),
        grid_spec=grid_spec,
        out_shape=jax.ShapeDtypeStruct((b, t, SB_W), BF16),
        compiler_params=_cparams(("parallel", "arbitrary")),
        name="sb_decode",
    )(page_table, q, new_kv, cache, z)


def _online_update(s, mask, m, l):
    sm = jnp.where(mask, s, NEG)
    m_new = jnp.maximum(m, jnp.max(sm, axis=1, keepdims=True))
    alpha = jnp.exp(m - m_new)
    p = jnp.where(mask, jnp.exp(sm - m_new), 0.0)
    l_new = alpha * l + jnp.sum(p, axis=1, keepdims=True)
    return p, alpha, m_new, l_new


def _normalise(acc, l):
    return acc / jnp.maximum(l, 1e-30)


def _diff_lambda(lq1_ref, lk1_ref, lq2_ref, lk2_ref, lambda_init):
    d1 = jnp.sum(lq1_ref[...] * lk1_ref[...], axis=1, keepdims=True)
    d2 = jnp.sum(lq2_ref[...] * lk2_ref[...], axis=1, keepdims=True)
    return jnp.exp(d1) - jnp.exp(d2) + lambda_init


def _diff_finish(o1, o2, lam, hg, zg, lambda_init):
    o = o1 - lam * o2
    ms = jnp.mean(o * o, axis=-1, keepdims=True)
    o = o * lax.rsqrt(ms + EPS) * hg * (1.0 - lambda_init)
    return o * _silu(zg)


def _diff_prompt_kernel(slopes_ref, q_ref, k_ref, v_ref, z_ref, lq1_ref, lk1_ref, lq2_ref, lk2_ref, hg_ref, o_ref,
                        *, tq, tk, lambda_init):
    h = pl.program_id(1)
    i = pl.program_id(2)
    scale = DIFF_HALF ** -0.5
    slope = slopes_ref[h]
    q = q_ref[0]
    q1 = q[:, :DIFF_HALF].astype(BF16)
    q2 = q[:, DIFF_HALF:].astype(BF16)
    qpos = i * tq + _iota((tq, 1), 0)
    n_chunks = (i + 1) * (tq // tk)

    def body(j, state):
        m1, l1, a1, m2, l2, a2 = state
        start = pl.multiple_of(j * tk, tk)
        k = k_ref[0, pl.ds(start, tk), :]
        v = v_ref[0, pl.ds(start, tk), :].astype(BF16)
        dist = qpos - (j * tk + _iota((1, tk), 1))
        mask = dist >= 0
        bias = slope * dist.astype(F32)
        s1 = _dot_nt(q1, k[:, :DIFF_HALF].astype(BF16)) * scale - bias
        s2 = _dot_nt(q2, k[:, DIFF_HALF:].astype(BF16)) * scale - bias
        p1, al1, m1, l1 = _online_update(s1, mask, m1, l1)
        p2, al2, m2, l2 = _online_update(s2, mask, m2, l2)
        a1 = al1 * a1 + _dot(p1.astype(BF16), v)
        a2 = al2 * a2 + _dot(p2.astype(BF16), v)
        return m1, l1, a1, m2, l2, a2

    m0 = jnp.full((tq, 1), NEG, F32)
    l0 = jnp.zeros((tq, 1), F32)
    a0 = jnp.zeros((tq, DIFF_VDIM), F32)
    m1, l1, a1, m2, l2, a2 = lax.fori_loop(0, n_chunks, body, (m0, l0, a0, m0, l0, a0))
    lam = _diff_lambda(lq1_ref, lk1_ref, lq2_ref, lk2_ref, lambda_init)
    out = _diff_finish(_normalise(a1, l1), _normalise(a2, l2), lam, hg_ref[...], z_ref[0], lambda_init)
    o_ref[0] = out.astype(o_ref.dtype)


def _smem_spec():
    return pl.BlockSpec(memory_space=pltpu.SMEM)


def diff_prompt(q, kv, z, lq1, lk1, lq2, lk2, head_g, lambda_init, *, tq=256, tk=128):
    b, t, _ = q.shape
    tq = min(tq, t)
    hq = pl.BlockSpec((1, tq, DIFF_VDIM), lambda bi, h, i: (bi, i, h))
    vec = lambda w: pl.BlockSpec((1, w), lambda bi, h, i: (0, 0))
    return pl.pallas_call(
        functools.partial(_diff_prompt_kernel, tq=tq, tk=tk, lambda_init=lambda_init),
        grid=(b, DIFF_HEADS, t // tq),
        in_specs=[
            _smem_spec(),
            hq,
            pl.BlockSpec((1, t, DIFF_VDIM), lambda bi, h, i: (bi, 0, h)),
            pl.BlockSpec((1, t, DIFF_VDIM), lambda bi, h, i: (bi, 0, DIFF_HEADS + h)),
            hq,
            vec(DIFF_HALF), vec(DIFF_HALF), vec(DIFF_HALF), vec(DIFF_HALF), vec(DIFF_VDIM),
        ],
        out_specs=hq,
        out_shape=jax.ShapeDtypeStruct((b, t, C_W), BF16),
        compiler_params=_cparams(("parallel", "parallel", "arbitrary")),
        name="diff_prompt",
    )(jnp.asarray(_alibi_slopes(DIFF_HEADS)), q, kv, kv, z,
      lq1.reshape(1, -1), lk1.reshape(1, -1), lq2.reshape(1, -1), lk2.reshape(1, -1), head_g.reshape(1, -1))


def _diff_decode_kernel(pt_ref, q_ref, new_ref, page_ref, z_ref, slope_ref, lq1_ref, lk1_ref, lq2_ref, lk2_ref,
                        hg_ref, o_ref, qbd_ref, m_ref, l_ref, acc_ref, *, t, n_pages, lambda_init):
    p = pl.program_id(1)
    scale = DIFF_HALF ** -0.5
    half = DIFF_HEADS * t
    rows = 2 * half
    tq = _iota((rows, 1), 0) % t
    past = n_pages * PAGE_SIZE
    col = _iota((1, PAGE_SIZE), 1)

    def attend(k_all, v_all, kpos, valid):
        s = _dot_nt(qbd_ref[...], k_all.astype(BF16)) * scale
        dist = (past + tq) - kpos
        mask = (dist >= 0) & valid
        s = s - slope_ref[...] * dist.astype(F32)
        pr, alpha, m_new, l_new = _online_update(s, mask, m_ref[...], l_ref[...])
        m_ref[...] = m_new
        l_ref[...] = l_new
        pr = pr.astype(BF16)
        for h in range(DIFF_HEADS):
            vh = v_all[:, h * DIFF_VDIM:(h + 1) * DIFF_VDIM].astype(BF16)
            for c in range(2):
                r = slice(c * half + h * t, c * half + (h + 1) * t)
                acc_ref[r, :] = alpha[r] * acc_ref[r, :] + _dot(pr[r], vh)

    @pl.when(p == 0)
    def _():
        q = q_ref[0]
        tiled = jnp.concatenate([q] * (2 * DIFF_HEADS), axis=0)
        r = _iota(tiled.shape, 0)
        want = 2 * ((r % half) // t) + r // half
        qbd_ref[...] = jnp.where(_iota(tiled.shape, 1) // DIFF_HALF == want, tiled, 0.0).astype(BF16)
        m_ref[...] = jnp.full_like(m_ref, NEG)
        l_ref[...] = jnp.zeros_like(l_ref)
        acc_ref[...] = jnp.zeros_like(acc_ref)

    @pl.when(p < n_pages)
    def _():
        attend(page_ref[:, :C_W], page_ref[:, C_W:], p * PAGE_SIZE + col, col >= 0)

    @pl.when(p == n_pages)
    def _():
        new = _pad_rows(new_ref[0], PAGE_SIZE)
        attend(new[:, :C_W], new[:, C_W:], past + col, col < t)
        lam = _diff_lambda(lq1_ref, lk1_ref, lq2_ref, lk2_ref, lambda_init)
        o1 = _normalise(acc_ref[:half, :], l_ref[:half, :])
        o2 = _normalise(acc_ref[half:, :], l_ref[half:, :])
        zg = z_ref[0]
        hg = hg_ref[...]
        for h in range(DIFF_HEADS):
            r = slice(h * t, (h + 1) * t)
            sl = slice(h * DIFF_VDIM, (h + 1) * DIFF_VDIM)
            o_ref[0, :, sl] = _diff_finish(o1[r], o2[r], lam, hg, zg[:, sl], lambda_init).astype(o_ref.dtype)


def diff_decode(q, new_kv, cache, page_table, z, lq1, lk1, lq2, lk2, head_g, lambda_init):
    b, t, _ = q.shape
    n_pages = page_table.shape[1]
    rows = 2 * DIFF_HEADS * t
    slopes = np.tile(np.repeat(_alibi_slopes(DIFF_HEADS), t), 2).reshape(rows, 1)
    tok = lambda w: pl.BlockSpec((1, t, w), lambda bi, p, pt: (bi, 0, 0))
    vec = lambda r, w: pl.BlockSpec((r, w), lambda bi, p, pt: (0, 0))
    grid_spec = pltpu.PrefetchScalarGridSpec(
        num_scalar_prefetch=1,
        grid=(b, n_pages + 1),
        in_specs=[
            tok(C_W),
            tok(2 * C_W),
            pl.BlockSpec((None, PAGE_SIZE, 2 * C_W),
                         lambda bi, p, pt: (pt[bi, jnp.minimum(p, n_pages - 1)], 0, 0)),
            tok(C_W),
            vec(rows, 1),
            vec(1, DIFF_HALF), vec(1, DIFF_HALF), vec(1, DIFF_HALF), vec(1, DIFF_HALF), vec(1, DIFF_VDIM),
        ],
        out_specs=tok(C_W),
        scratch_shapes=[
            pltpu.VMEM((rows, C_W), BF16),
            pltpu.VMEM((rows, 1), F32),
            pltpu.VMEM((rows, 1), F32),
            pltpu.VMEM((rows, DIFF_VDIM), F32),
        ],
    )
    return pl.pallas_call(
        functools.partial(_diff_decode_kernel, t=t, n_pages=n_pages, lambda_init=lambda_init),
        grid_spec=grid_spec,
        out_shape=jax.ShapeDtypeStruct((b, t, C_W), BF16),
        compiler_params=_cparams(("parallel", "arbitrary")),
        name="diff_decode",
    )(page_table, q, new_kv, cache, z, jnp.asarray(slopes),
      lq1.reshape(1, -1), lk1.reshape(1, -1), lq2.reshape(1, -1), lk2.reshape(1, -1), head_g.reshape(1, -1))


def _masked_softmax_parts(parts):
    m = None
    for s, mask in parts:
        pm = jnp.max(jnp.where(mask, s, NEG), axis=1, keepdims=True)
        m = pm if m is None else jnp.maximum(m, pm)
    es = [jnp.where(mask, jnp.exp(jnp.where(mask, s, NEG) - m), 0.0) for s, mask in parts]
    den = None
    for e in es:
        d = jnp.sum(e, axis=1, keepdims=True)
        den = d if den is None else den + d
    den = jnp.maximum(den, 1e-30)
    return [e / den for e in es]


def _nsa_prompt_kernel(slopes_ref, q_ref, kc_ref, vc_ref, ks_ref, vs_ref, kw_ref, vw_ref, lg_ref, z_ref,
                       wk_ref, wv_ref, expand_ref, o_ref, kcmp_ref, vcmp_ref, selmask_ref, *, tq, t_total):
    g = pl.program_id(1)
    i = pl.program_id(2)
    scale = HEAD_DIM ** -0.5
    nb = t_total // SEL_BLOCK
    rows = NSA_GROUP * tq
    tk = tq

    @pl.when(i == 0)
    def _():
        for src, w_ref, dst in ((kc_ref, wk_ref, kcmp_ref), (vc_ref, wv_ref, vcmp_ref)):
            x = src[0].reshape(nb, SEL_BLOCK, HEAD_DIM)
            w = w_ref[0][None]
            even = jnp.sum(x[:, :CMP_BLOCK, :] * w, axis=1)
            odd = jnp.sum(x[:, CMP_BLOCK:, :] * w, axis=1)
            pad = jnp.zeros((LANES - 2 * nb, HEAD_DIM), F32)
            dst[...] = jnp.concatenate([even, odd, pad], axis=0).astype(BF16)

    q = q_ref[0]
    q4 = jnp.concatenate([q[:, zz * HEAD_DIM:(zz + 1) * HEAD_DIM] for zz in range(NSA_GROUP)], axis=0).astype(BF16)
    qpos1 = i * tq + _iota((tq, 1), 0)
    qpos4 = jnp.concatenate([qpos1] * NSA_GROUP, axis=0)
    slope4 = jnp.concatenate(
        [jnp.full((tq, 1), slopes_ref[g * NSA_GROUP + zz], F32) for zz in range(NSA_GROUP)], axis=0)
    col = _iota((1, LANES), 1)

    cidx = jnp.where(col < nb, 2 * col, 2 * (col - nb) + 1)
    dist_c = qpos4 - (cidx * CMP_BLOCK + (CMP_BLOCK - 1))
    s_c = _dot_nt(q4, kcmp_ref[...]) * scale - slope4 * dist_c.astype(F32)
    (p_c,) = _masked_softmax_parts([(s_c, (dist_c >= 0) & (col < 2 * nb))])
    o_cmp = _dot(p_c.astype(BF16), vcmp_ref[...])
    imp = p_c[0:tq]
    for zz in range(1, NSA_GROUP):
        imp = imp + p_c[zz * tq:(zz + 1) * tq]
    pair = imp + pltpu.roll(imp, LANES - nb, 1)

    cur = qpos1 // SEL_BLOCK
    valid = col * SEL_BLOCK <= qpos1
    forced = (col == cur) | (col == 0)
    score = jnp.where(forced, FORCED_SCORE, jnp.where(valid, pair, -1.0))
    score = jnp.where(col < nb, score, -2.0)
    score_t = score.T
    cand = score_t[:nb]
    blk = _iota((nb, 1), 0)
    rank = jnp.zeros((nb, tq), F32)
    for j in range(nb):
        r = score_t[j:j + 1, :]
        ge = jnp.where(r >= cand, 1.0, 0.0)
        gt = jnp.where(r > cand, 1.0, 0.0)
        rank = rank + jnp.where(blk > j, ge, gt)
    sel_t = (rank < float(min(SEL_TOPK, nb))).astype(F32)
    sel = jnp.concatenate([sel_t, jnp.zeros((LANES - nb, tq), F32)], axis=0).T
    selmask_ref[...] = _dot(sel.astype(BF16), expand_ref[...])

    def sweep(k_ref, v_ref, lo, hi, mask_fn):
        def body(j, state):
            m, l, acc = state
            start = pl.multiple_of(j * tk, tk)
            k = k_ref[0, pl.ds(start, tk), :].astype(BF16)
            v = v_ref[0, pl.ds(start, tk), :].astype(BF16)
            dist1 = qpos1 - (j * tk + _iota((1, tk), 1))
            mask1 = mask_fn(start, dist1)
            dist4 = jnp.concatenate([dist1] * NSA_GROUP, axis=0)
            mask4 = jnp.concatenate([mask1] * NSA_GROUP, axis=0)
            s = _dot_nt(q4, k) * scale - slope4 * dist4.astype(F32)
            p, alpha, m, l = _online_update(s, mask4, m, l)
            acc = alpha * acc + _dot(p.astype(BF16), v)
            return m, l, acc

        m0 = jnp.full((rows, 1), NEG, F32)
        l0 = jnp.zeros((rows, 1), F32)
        a0 = jnp.zeros((rows, HEAD_DIM), F32)
        m, l, acc = lax.fori_loop(lo, hi, body, (m0, l0, a0))
        return _normalise(acc, l)

    def slc_mask(start, dist1):
        return (selmask_ref[:, pl.ds(start, tk)] > 0.5) & (dist1 >= 0)

    def win_mask(start, dist1):
        return (dist1 >= 0) & (dist1 < WINDOW)

    o_slc = sweep(ks_ref, vs_ref, 0, i + 1, slc_mask)
    o_win = sweep(kw_ref, vw_ref, jnp.maximum(i - WINDOW // tk, 0), i + 1, win_mask)

    gates = pltpu.roll(_sigmoid(lg_ref[0]), (LANES - 3 * NSA_GROUP * g) % LANES, 1)
    zg = z_ref[0]
    for zz in range(NSA_GROUP):
        r = slice(zz * tq, (zz + 1) * tq)
        mix = (gates[:, 3 * zz:3 * zz + 1] * o_cmp[r] + gates[:, 3 * zz + 1:3 * zz + 2] * o_slc[r]
               + gates[:, 3 * zz + 2:3 * zz + 3] * o_win[r])
        sl = slice(zz * HEAD_DIM, (zz + 1) * HEAD_DIM)
        o_ref[0, :, sl] = (mix * _silu(zg[:, sl])).astype(o_ref.dtype)


def _cmp_weight_rows(w):
    return jnp.broadcast_to(w.T[:, :, None], (NSA_KV_HEADS, CMP_BLOCK, HEAD_DIM)).astype(F32)


def nsa_prompt(q, nskv, win, logits, z, cmp_wk, cmp_wv, *, tq=128):
    b, t, _ = q.shape
    nb = t // SEL_BLOCK
    assert t % tq == 0 and 2 * nb <= LANES and WINDOW % tq == 0
    gw = NSA_GROUP * HEAD_DIM
    expand = (np.arange(LANES)[:, None] == (np.arange(t)[None, :] // SEL_BLOCK)).astype(np.float32)
    kvspec = lambda kind: pl.BlockSpec((1, t, HEAD_DIM), lambda bi, g, i: (bi, 0, kind * NSA_KV_HEADS + g))
    qspec = pl.BlockSpec((1, tq, gw), lambda bi, g, i: (bi, i, g))
    wspec = pl.BlockSpec((1, CMP_BLOCK, HEAD_DIM), lambda bi, g, i: (g, 0, 0))
    return pl.pallas_call(
        functools.partial(_nsa_prompt_kernel, tq=tq, t_total=t),
        grid=(b, NSA_KV_HEADS, t // tq),
        in_specs=[
            _smem_spec(),
            qspec,
            kvspec(0), kvspec(1), kvspec(2), kvspec(3),
            kvspec(0), kvspec(1),
            pl.BlockSpec((1, tq, LANES), lambda bi, g, i: (bi, i, 0)),
            qspec,
            wspec, wspec,
            pl.BlockSpec((LANES, t), lambda bi, g, i: (0, 0)),
        ],
        out_specs=qspec,
        out_shape=jax.ShapeDtypeStruct((b, t, NSA_W), BF16),
        scratch_shapes=[
            pltpu.VMEM((LANES, HEAD_DIM), BF16),
            pltpu.VMEM((LANES, HEAD_DIM), BF16),
            pltpu.VMEM((tq, t), F32),
        ],
        compiler_params=_cparams(("parallel", "parallel", "arbitrary")),
        name="nsa_prompt",
    )(jnp.asarray(_alibi_slopes(NSA_HEADS)), q, nskv, nskv, nskv, nskv, win, win, logits, z,
      _cmp_weight_rows(cmp_wk), _cmp_weight_rows(cmp_wv), jnp.asarray(expand, BF16))


def _nsa_compress_kernel(pt_ref, page_ref, w_ref, even_ref, odd_ref):
    x = page_ref[...].reshape(PAGE_SIZE // CMP_BLOCK, CMP_BLOCK, 2 * NSA_KV_W)
    c = jnp.sum(x * w_ref[...][None], axis=1)
    even_ref[0, 0] = jnp.concatenate([c[r:r + 1] for r in range(0, c.shape[0], 2)], axis=0)
    odd_ref[0, 0] = jnp.concatenate([c[r:r + 1] for r in range(1, c.shape[0], 2)], axis=0)


def nsa_compress_pages(cache, page_table, cmp_wk, cmp_wv):
    b, n_pages = page_table.shape
    per_page = PAGE_SIZE // CMP_BLOCK // 2
    w = jnp.concatenate([jnp.repeat(cmp_wk, HEAD_DIM, axis=1), jnp.repeat(cmp_wv, HEAD_DIM, axis=1)], axis=1)
    grid_spec = pltpu.PrefetchScalarGridSpec(
        num_scalar_prefetch=1,
        grid=(b, n_pages),
        in_specs=[
            pl.BlockSpec((None, PAGE_SIZE, 2 * NSA_KV_W), lambda bi, p, pt: (pt[bi, p], 0, 0)),
            pl.BlockSpec((CMP_BLOCK, 2 * NSA_KV_W), lambda bi, p, pt: (0, 0)),
        ],
        out_specs=[pl.BlockSpec((1, 1, per_page, 2 * NSA_KV_W), lambda bi, p, pt: (bi, p, 0, 0))] * 2,
    )
    even, odd = pl.pallas_call(
        _nsa_compress_kernel,
        grid_spec=grid_spec,
        out_shape=[jax.ShapeDtypeStruct((b, n_pages, per_page, 2 * NSA_KV_W), F32)] * 2,
        compiler_params=_cparams(("parallel", "parallel")),
        name="nsa_compress_pages",
    )(page_table, cache, w.astype(F32))
    return (even.reshape(b, n_pages * per_page, 2 * NSA_KV_W), odd.reshape(b, n_pages * per_page, 2 * NSA_KV_W))


def _nsa_decode_kernel(pt_ref, q_ref, even_ref, odd_ref, page_ref, new_ref, wst_ref, wnew_ref, lg_ref, z_ref, slope_ref,
                       o_ref, qbd_ref, sel_ref, ocmp_ref, m_ref, l_ref, acc_ref, *, t, n_pages):
    p = pl.program_id(1)
    scale = HEAD_DIM ** -0.5
    rows = NSA_HEADS * t
    grp_rows = NSA_GROUP * t
    past = n_pages * PAGE_SIZE
    n_blk = past // SEL_BLOCK
    half = n_blk
    tq = _iota((rows, 1), 0) % t
    qpos = past + tq
    slope = slope_ref[...]
    col = _iota((1, PAGE_SIZE), 1)

    def group_pv(pr, v_all):
        return jnp.concatenate(
            [_dot(pr[gg * grp_rows:(gg + 1) * grp_rows], v_all[:, gg * HEAD_DIM:(gg + 1) * HEAD_DIM].astype(BF16))
             for gg in range(NSA_KV_HEADS)], axis=0)

    @pl.when(p == 0)
    def _():
        qbd = _block_diag_rows_grouped(q_ref[0], t).astype(BF16)
        qbd_ref[...] = qbd
        parts, vals = [], []
        for par, cmp_ref in enumerate((even_ref, odd_ref)):
            kc = cmp_ref[0, :, :NSA_KV_W].astype(BF16)
            vals.append(cmp_ref[0, :, NSA_KV_W:].astype(BF16))
            c_end = (2 * _iota((1, half), 1) + par) * CMP_BLOCK + (CMP_BLOCK - 1)
            dist = qpos - c_end
            parts.append((_dot_nt(qbd, kc) * scale - slope * dist.astype(F32), dist >= 0))
        p_e, p_o = _masked_softmax_parts(parts)
        ocmp_ref[...] = group_pv(p_e.astype(BF16), vals[0]) + group_pv(p_o.astype(BF16), vals[1])
        pe = p_e + p_o
        blkcol = _iota((1, n_blk), 1)
        picked = []
        for gg in range(NSA_KV_HEADS):
            imp = pe[gg * grp_rows:gg * grp_rows + t]
            for zz in range(1, NSA_GROUP):
                imp = imp + pe[gg * grp_rows + zz * t:gg * grp_rows + (zz + 1) * t]
            sc = jnp.where(blkcol == 0, -1.0, imp)
            sel = (blkcol == 0)
            for _ in range(min(SEL_TOPK, n_blk + 1) - 2):
                mx = jnp.max(sc, axis=1, keepdims=True)
                first = jnp.min(jnp.where(sc == mx, blkcol, n_blk), axis=1, keepdims=True)
                hit = blkcol == first
                sel = sel | hit
                sc = jnp.where(hit, -1.0, sc)
            self32 = sel.astype(F32)
            picked.extend([self32] * NSA_GROUP)
        sel_ref[...] = jnp.concatenate(picked, axis=0).astype(BF16)
        m_ref[...] = jnp.full_like(m_ref, NEG)
        l_ref[...] = jnp.zeros_like(l_ref)
        acc_ref[...] = jnp.zeros_like(acc_ref)

    def attend(k_all, v_all, dist, mask):
        s = _dot_nt(qbd_ref[...], k_all.astype(BF16)) * scale - slope * dist.astype(F32)
        pr, alpha, m_new, l_new = _online_update(s, mask, m_ref[...], l_ref[...])
        m_ref[...] = m_new
        l_ref[...] = l_new
        acc_ref[...] = alpha * acc_ref[...] + group_pv(pr.astype(BF16), v_all)

    @pl.when(p < n_pages)
    def _():
        per_page = PAGE_SIZE // SEL_BLOCK
        expand = (_iota((n_blk, PAGE_SIZE), 0) == per_page * p + _iota((n_blk, PAGE_SIZE), 1) // SEL_BLOCK)
        chosen = _dot(sel_ref[...], expand.astype(BF16)) > 0.5
        dist = qpos - (p * PAGE_SIZE + col)
        attend(page_ref[:, :NSA_KV_W], page_ref[:, NSA_KV_W:], dist, chosen & (dist >= 0))

    @pl.when(p == n_pages)
    def _():
        new = _pad_rows(new_ref[0], PAGE_SIZE)
        dist = qpos - (past + col)
        attend(new[:, 2 * NSA_KV_W:3 * NSA_KV_W], new[:, 3 * NSA_KV_W:], dist, (dist >= 0) & (col < t))
        o_slc = _normalise(acc_ref[...], l_ref[...])
        n_state = wst_ref.shape[1]
        qbd = qbd_ref[...]
        dist_s = qpos - (past - n_state + _iota((1, n_state), 1))
        s_s = _dot_nt(qbd, wst_ref[0, :, :NSA_KV_W].astype(BF16)) * scale - slope * dist_s.astype(F32)
        wnew = _pad_rows(wnew_ref[0], PAGE_SIZE)
        s_n = _dot_nt(qbd, wnew[:, :NSA_KV_W].astype(BF16)) * scale - slope * dist.astype(F32)
        p_s, p_n = _masked_softmax_parts([
            (s_s, (dist_s >= 0) & (dist_s < WINDOW)),
            (s_n, (dist >= 0) & (dist < WINDOW) & (col < t)),
        ])
        o_win = group_pv(p_s.astype(BF16), wst_ref[0, :, NSA_KV_W:]) + group_pv(p_n.astype(BF16), wnew[:, NSA_KV_W:])
        o_cmp = ocmp_ref[...]
        gates = _sigmoid(lg_ref[0])
        zg = z_ref[0]
        for h in range(NSA_HEADS):
            r = slice(h * t, (h + 1) * t)
            mix = (gates[:, 3 * h:3 * h + 1] * o_cmp[r] + gates[:, 3 * h + 1:3 * h + 2] * o_slc[r]
                   + gates[:, 3 * h + 2:3 * h + 3] * o_win[r])
            sl = slice(h * HEAD_DIM, (h + 1) * HEAD_DIM)
            o_ref[0, :, sl] = (mix * _silu(zg[:, sl])).astype(o_ref.dtype)


def _block_diag_rows_grouped(q, t):
    pieces = []
    for h in range(NSA_HEADS):
        qh = q[:, h * HEAD_DIM:(h + 1) * HEAD_DIM]
        gg = h // NSA_GROUP
        zero = jnp.zeros_like(qh)
        pieces.append(jnp.concatenate([qh if c == gg else zero for c in range(NSA_KV_HEADS)], axis=1))
    return jnp.concatenate(pieces, axis=0)


def nsa_decode(q, cmp_even, cmp_odd, cache, page_table, new_nskv, win_state, win_new, logits, z):
    b, t, _ = q.shape
    n_pages = page_table.shape[1]
    past = n_pages * PAGE_SIZE
    n_blk = past // SEL_BLOCK
    n_state = win_state.shape[1]
    assert past % SEL_BLOCK == 0 and t < CMP_BLOCK and n_blk + 1 >= SEL_TOPK and n_state >= WINDOW - 1
    rows = NSA_HEADS * t
    slopes = np.repeat(_alibi_slopes(NSA_HEADS), t).reshape(rows, 1)
    tok = lambda w: pl.BlockSpec((1, t, w), lambda bi, p, pt: (bi, 0, 0))
    grid_spec = pltpu.PrefetchScalarGridSpec(
        num_scalar_prefetch=1,
        grid=(b, n_pages + 1),
        in_specs=[
            tok(NSA_W),
            pl.BlockSpec((1, n_blk, 2 * NSA_KV_W), lambda bi, p, pt: (bi, 0, 0)),
            pl.BlockSpec((1, n_blk, 2 * NSA_KV_W), lambda bi, p, pt: (bi, 0, 0)),
            pl.BlockSpec((None, PAGE_SIZE, 2 * NSA_KV_W),
                         lambda bi, p, pt: (pt[bi, jnp.minimum(p, n_pages - 1)], 0, 1)),
            tok(4 * NSA_KV_W),
            pl.BlockSpec((1, n_state, 2 * NSA_KV_W), lambda bi, p, pt: (bi, 0, 0)),
            tok(2 * NSA_KV_W),
            tok(LANES),
            tok(NSA_W),
            pl.BlockSpec((rows, 1), lambda bi, p, pt: (0, 0)),
        ],
        out_specs=tok(NSA_W),
        scratch_shapes=[
            pltpu.VMEM((rows, NSA_KV_W), BF16),
            pltpu.VMEM((rows, n_blk), BF16),
            pltpu.VMEM((rows, HEAD_DIM), F32),
            pltpu.VMEM((rows, 1), F32),
            pltpu.VMEM((rows, 1), F32),
            pltpu.VMEM((rows, HEAD_DIM), F32),
        ],
    )
    return pl.pallas_call(
        functools.partial(_nsa_decode_kernel, t=t, n_pages=n_pages),
        grid_spec=grid_spec,
        out_shape=jax.ShapeDtypeStruct((b, t, NSA_W), BF16),
        compiler_params=_cparams(("parallel", "arbitrary")),
        name="nsa_decode",
    )(page_table, q, cmp_even, cmp_odd, cache, new_nskv, win_state, win_new, logits, z, jnp.asarray(slopes))


def _ab_weights(w_in):
    o = np.cumsum([0, SB_W, 2 * SB_W, SB_W, NSA_W, 4 * NSA_KV_W, 2 * NSA_KV_W, 3 * NSA_HEADS, NSA_W])
    cut = lambda a, b_: w_in[:, a:b_].astype(BF16)
    logits = jnp.pad(w_in[:, o[6]:o[7]], ((0, 0), (0, LANES - 3 * NSA_HEADS))).astype(BF16)
    return dict(sbq=cut(o[0], o[1]), sbkv=cut(o[1], o[2]), sbz=cut(o[2], o[3]), nsq=cut(o[3], o[4]),
                nskv=cut(o[4], o[5]), win=cut(o[5], o[6]), logits=logits, nsz=cut(o[7], o[8]))


def _c_weights(w_in):
    cut = lambda a, b_: w_in[:, a:b_].astype(BF16)
    return dict(q=cut(0, C_W), kv=cut(C_W, 3 * C_W), z=cut(3 * C_W, 4 * C_W))


def kernel(x_prompt, x_sample, p_prompt, p_sample, cache_sb_kv, cache_nsa_kv, state_nsa_win_kv, cache_diff_kv,
           page_table, norm_g, w_in_ab, nsa_cmp_wk, nsa_cmp_wv, w_out_ab, w_in_c, diff_lq1, diff_lk1, diff_lq2,
           diff_lk2, diff_head_g, w_out_c, ple_norm_g, w_ple_gate, w_ple_proj, final_norm_g):
    bp, tp, d = x_prompt.shape
    bs, ts, _ = x_sample.shape
    depth = norm_g.shape[0]
    n_phys = cache_sb_kv.shape[1]

    def run(x, p_emb, sample):
        b, t, _ = x.shape
        h = x.reshape(b * t, d)
        sb_rows, nsa_rows, win_rows, diff_rows = [], [], [], []
        for i in range(depth):
            j = i // 2
            hn = rmsnorm(h, norm_g[i], BF16)
            if i % 2 == 0:
                w = _ab_weights(w_in_ab[j])
                proj = {name: matmul(hn, wm).reshape(b, t, -1) for name, wm in w.items()}
                if sample:
                    sb_mixed = sb_decode(proj["sbq"], proj["sbkv"], cache_sb_kv[j].reshape(n_phys, PAGE_SIZE, -1),
                                         page_table, proj["sbz"])
                    nsa_cache = cache_nsa_kv[j].reshape(n_phys, PAGE_SIZE, -1)
                    cmp_even, cmp_odd = nsa_compress_pages(nsa_cache, page_table, nsa_cmp_wk[j], nsa_cmp_wv[j])
                    win_state = state_nsa_win_kv[j].reshape(b, -1, 2 * NSA_KV_W)
                    ns_mixed = nsa_decode(proj["nsq"], cmp_even, cmp_odd, nsa_cache, page_table, proj["nskv"], win_state,
                                          proj["win"], proj["logits"], proj["nsz"])
                    win_all = jnp.concatenate([win_state, proj["win"]], axis=1)
                else:
                    sb_mixed = sb_prompt(proj["sbq"], proj["sbkv"], proj["sbz"])
                    ns_mixed = nsa_prompt(proj["nsq"], proj["nskv"], proj["win"], proj["logits"], proj["nsz"],
                                          nsa_cmp_wk[j], nsa_cmp_wv[j])
                    win_all = proj["win"]
                mixed = jnp.concatenate([sb_mixed, ns_mixed], axis=-1).reshape(b * t, -1)
                h = matmul(mixed, w_out_ab[j].astype(BF16), residual=h)
                sb_rows.append(proj["sbkv"].reshape(b, t, 2, SB_HEADS, HEAD_DIM))
                nsa_rows.append(proj["nskv"].reshape(b, t, 4, NSA_KV_HEADS, HEAD_DIM))
                keep = min(WINDOW, win_all.shape[1])
                win_rows.append(win_all[:, win_all.shape[1] - keep:].reshape(b, keep, 2, NSA_KV_HEADS, HEAD_DIM))
            else:
                lambda_init = 0.8 - 0.6 * math.exp(-0.3 * i)
                w = _c_weights(w_in_c[j])
                proj = {name: matmul(hn, wm).reshape(b, t, -1) for name, wm in w.items()}
                lam_args = (diff_lq1[j], diff_lk1[j], diff_lq2[j], diff_lk2[j], diff_head_g[j], lambda_init)
                if sample:
                    mixed = diff_decode(proj["q"], proj["kv"], cache_diff_kv[j].reshape(n_phys, PAGE_SIZE, -1),
                                        page_table, proj["z"], *lam_args)
                else:
                    mixed = diff_prompt(proj["q"], proj["kv"], proj["z"], *lam_args)
                h = matmul(mixed.reshape(b * t, -1), w_out_c[j].astype(BF16), residual=h)
                diff_rows.append(proj["kv"].reshape(b, t, 2, DIFF_HEADS, DIFF_VDIM))
            hn2 = rmsnorm(h, ple_norm_g[i], BF16)
            h = ple(hn2, w_ple_gate[i].astype(BF16), p_emb[i].reshape(b * t, -1).astype(BF16),
                    w_ple_proj[i].astype(BF16), h)
        y = rmsnorm(h, final_norm_g, F32).reshape(b, t, d)
        return y, jnp.stack(sb_rows), jnp.stack(nsa_rows), jnp.stack(win_rows), jnp.stack(diff_rows)

    y_p, sb_p, nsa_p, win_p, diff_p = run(x_prompt, p_prompt, False)
    y_s, sb_s, nsa_s, win_s, diff_s = run(x_sample, p_sample, True)
    return (y_p, y_s, sb_p, sb_s, nsa_p, nsa_s, win_p, win_s, diff_p, diff_s)
```

```python
import functools
import math

import jax
import jax.numpy as jnp
import numpy as np
from jax import lax
from jax.experimental import pallas as pl
from jax.experimental.pallas import tpu as pltpu

F32 = jnp.float32
BF16 = jnp.bfloat16

HEAD_DIM = 128
SB_HEADS = 16
NSA_HEADS = 16
NSA_KV_HEADS = 4
NSA_GROUP = NSA_HEADS // NSA_KV_HEADS
CMP_BLOCK = 32
SEL_BLOCK = 64
SEL_TOPK = 16
WINDOW = 512
DIFF_HEADS = 16
DIFF_HALF = 128
DIFF_VDIM = 2 * DIFF_HALF
PAGE_SIZE = 128
EPS = 1e-6
NEG = -1e30
FORCED_SCORE = 1e4

SB_W = SB_HEADS * HEAD_DIM
NSA_W = NSA_HEADS * HEAD_DIM
NSA_KV_W = NSA_KV_HEADS * HEAD_DIM
C_W = DIFF_HEADS * DIFF_VDIM
LANES = 128
SUBLANES = 8
EXP_UNDERFLOW = -110.0
VMEM_LIMIT = 56 * 1024 * 1024


def _alibi_slopes(n):
    return np.asarray(2.0 ** (-8.0 * np.arange(1, n + 1) / n), dtype=np.float32)


def _cparams(sem):
    return pltpu.CompilerParams(dimension_semantics=sem, vmem_limit_bytes=VMEM_LIMIT)


def _dot(a, b):
    return jnp.dot(a, b, preferred_element_type=F32)


def _dot_nt(a, b):
    return lax.dot_general(a, b, (((1,), (1,)), ((), ())), preferred_element_type=F32)


def _silu(z):
    return z * (1.0 / (1.0 + jnp.exp(-z)))


def _sigmoid(z):
    return 1.0 / (1.0 + jnp.exp(-z))


def _iota(shape, dim):
    return lax.broadcasted_iota(jnp.int32, shape, dim)


def _rmsnorm_kernel(x_ref, g_ref, o_ref):
    x = x_ref[...]
    ms = jnp.mean(x * x, axis=-1, keepdims=True)
    o_ref[...] = (x * lax.rsqrt(ms + EPS) * g_ref[...]).astype(o_ref.dtype)


def rmsnorm(x, g, out_dtype):
    m, d = x.shape
    tm = min(m, 256)
    return pl.pallas_call(
        _rmsnorm_kernel,
        grid=(m // tm,),
        in_specs=[pl.BlockSpec((tm, d), lambda i: (i, 0)), pl.BlockSpec((1, d), lambda i: (0, 0))],
        out_specs=pl.BlockSpec((tm, d), lambda i: (i, 0)),
        out_shape=jax.ShapeDtypeStruct((m, d), out_dtype),
        compiler_params=_cparams(("parallel",)),
        name="rmsnorm",
    )(x, g.reshape(1, d))


def _mm_kernel(a_ref, w_ref, o_ref):
    o_ref[...] = _dot(a_ref[...], w_ref[...]).astype(o_ref.dtype)


def _mm_res_kernel(a_ref, w_ref, r_ref, o_ref):
    o_ref[...] = r_ref[...] + _dot(a_ref[...], w_ref[...])


def matmul(a, w, residual=None, out_dtype=F32):
    m, k = a.shape
    n = w.shape[1]
    tm = min(m, 1024)
    tn = min(n, 512)
    grid = (m // tm, n // tn)
    in_specs = [pl.BlockSpec((tm, k), lambda i, j: (i, 0)), pl.BlockSpec((k, tn), lambda i, j: (0, j))]
    args = [a, w]
    kern = _mm_kernel
    if residual is not None:
        in_specs.append(pl.BlockSpec((tm, tn), lambda i, j: (i, j)))
        args.append(residual)
        kern = _mm_res_kernel
    return pl.pallas_call(
        kern,
        grid=grid,
        in_specs=in_specs,
        out_specs=pl.BlockSpec((tm, tn), lambda i, j: (i, j)),
        out_shape=jax.ShapeDtypeStruct((m, n), out_dtype),
        compiler_params=_cparams(("parallel", "parallel")),
        name="matmul",
    )(*args)


def _ple_kernel(hn_ref, wg_ref, p_ref, wp_ref, h_ref, o_ref):
    gate = _sigmoid(_dot(hn_ref[...], wg_ref[...]))
    proj = _dot(p_ref[...], wp_ref[...])
    o_ref[...] = h_ref[...] + gate * proj


def ple(hn, wg, p, wp, h):
    m, d = hn.shape
    n = wg.shape[1]
    pd = p.shape[1]
    tm = min(m, 1024)
    tn = min(n, 512)
    return pl.pallas_call(
        _ple_kernel,
        grid=(m // tm, n // tn),
        in_specs=[
            pl.BlockSpec((tm, d), lambda i, j: (i, 0)),
            pl.BlockSpec((d, tn), lambda i, j: (0, j)),
            pl.BlockSpec((tm, pd), lambda i, j: (i, 0)),
            pl.BlockSpec((pd, tn), lambda i, j: (0, j)),
            pl.BlockSpec((tm, tn), lambda i, j: (i, j)),
        ],
        out_specs=pl.BlockSpec((tm, tn), lambda i, j: (i, j)),
        out_shape=jax.ShapeDtypeStruct((m, n), F32),
        compiler_params=_cparams(("parallel", "parallel")),
        name="ple",
    )(hn, wg, p, wp, h)


def _suffix_sum_matrix(c):
    return (_iota((c, c), 0) >= _iota((c, c), 1)).astype(BF16)


def _sb_prompt_kernel(q_ref, k_ref, v_ref, z_ref, o_ref, carry_ref, acc_ref, *, tq, tk):
    i = pl.program_id(2)
    qb = (q_ref[0] * HEAD_DIM ** -0.5).astype(BF16)
    incl_mat = _suffix_sum_matrix(tk)
    rc = _iota((tq, tk), 0) - _iota((tq, tk), 1)
    carry_ref[...] = jnp.zeros_like(carry_ref)
    acc_ref[...] = jnp.zeros_like(acc_ref)

    def chunk(j, masked):
        start = pl.multiple_of(j * tk, tk)
        k = k_ref[0, pl.ds(start, tk), :].astype(BF16)
        v = v_ref[0, pl.ds(start, tk), :].astype(BF16)
        z = _dot_nt(qb, k)
        lb = jnp.minimum(z, 0.0) - jnp.log(1.0 + jnp.exp(-jnp.abs(z)))
        lk = lb - z
        if masked:
            mask = rc > j * tk - i * tq
            lb = jnp.where(mask, lb, 0.0)
            lk = jnp.where(mask, lk, 0.0)
        hi = lk.astype(BF16)
        lo = (lk - hi.astype(F32)).astype(BF16)
        incl = _dot(hi, incl_mat) + _dot(lo, incl_mat)
        carry = carry_ref[...]
        a = jnp.exp(lb + (incl - lk + carry))
        if masked:
            a = jnp.where(mask, a, 0.0)
        acc_ref[...] += _dot(a.astype(BF16), v)
        carry_ref[...] = carry + incl[:, 0:1]

    n_diag = tq // tk
    top = (i + 1) * n_diag - 1
    for d in range(n_diag):
        chunk(top - d, True)

    def body(state):
        j, _ = state
        chunk(j, False)
        live = jnp.max(carry_ref[...]) > EXP_UNDERFLOW
        return j - 1, live.astype(jnp.int32)

    lax.while_loop(lambda st: (st[0] >= 0) & (st[1] > 0), body, (top - n_diag, jnp.int32(1)))
    o_ref[0] = (acc_ref[...] * _silu(z_ref[0])).astype(o_ref.dtype)


def sb_prompt(q, kv, z, *, tq=512, tk=256):
    b, t, _ = q.shape
    tq = min(tq, t)
    tk = min(tk, tq)
    hq = pl.BlockSpec((1, tq, HEAD_DIM), lambda bi, h, i: (bi, i, h))
    return pl.pallas_call(
        functools.partial(_sb_prompt_kernel, tq=tq, tk=tk),
        grid=(b, SB_HEADS, t // tq),
        in_specs=[
            hq,
            pl.BlockSpec((1, t, HEAD_DIM), lambda bi, h, i: (bi, 0, h)),
            pl.BlockSpec((1, t, HEAD_DIM), lambda bi, h, i: (bi, 0, SB_HEADS + h)),
            hq,
        ],
        out_specs=hq,
        out_shape=jax.ShapeDtypeStruct((b, t, SB_W), BF16),
        scratch_shapes=[pltpu.VMEM((tq, 1), F32), pltpu.VMEM((tq, HEAD_DIM), F32)],
        compiler_params=_cparams(("parallel", "parallel", "arbitrary")),
        name="sb_prompt",
    )(q, kv, kv, z)


def _pad_rows(x, rows):
    return jnp.concatenate([x, jnp.zeros((rows - x.shape[0], x.shape[1]), x.dtype)], axis=0)


def _sb_decode_kernel(pt_ref, q_ref, new_ref, z_ref, cache_ref, o_ref, buf_ref, sem_ref, carry_ref, acc_ref,
                      *, t, n_pages, page_off):
    b = pl.program_id(0)
    scale = HEAD_DIM ** -0.5
    rows = SB_HEADS * t
    tiles_per_key = 2 * SB_HEADS // SUBLANES
    tq = _iota((rows, 1), 0) % t
    col = _iota((1, PAGE_SIZE), 1)
    qb = q_ref[0].astype(BF16)

    def page_copies(slot, n):
        page = page_off + pt_ref[b, n_pages - 1 - n]
        return [pltpu.make_async_copy(cache_ref.at[page, :, pl.ds(r * SUBLANES, SUBLANES), :],
                                      buf_ref.at[slot, r], sem_ref.at[slot]) for r in range(tiles_per_key)]

    def attend(get_k, get_v, mask):
        z = jnp.concatenate([_dot_nt(qb[:, h * HEAD_DIM:(h + 1) * HEAD_DIM], get_k(h).astype(BF16))
                             for h in range(SB_HEADS)], axis=0) * scale
        carry = carry_ref[...]
        lb = jnp.minimum(z, 0.0) - jnp.log(1.0 + jnp.exp(-jnp.abs(z)))
        lk = lb - z
        if mask is not None:
            lb = jnp.where(mask, lb, 0.0)
            lk = jnp.where(mask, lk, 0.0)
        incl_mat = _suffix_sum_matrix(PAGE_SIZE)
        hi = lk.astype(BF16)
        lo = (lk - hi.astype(F32)).astype(BF16)
        incl = _dot(hi, incl_mat) + _dot(lo, incl_mat)
        a = jnp.exp(lb + (incl - lk + carry))
        if mask is not None:
            a = jnp.where(mask, a, 0.0)
        a = a.astype(BF16)
        for h in range(SB_HEADS):
            acc_ref[h * t:(h + 1) * t, :] += _dot(a[h * t:(h + 1) * t, :], get_v(h).astype(BF16))
        carry_ref[...] = carry + incl[:, 0:1]

    carry_ref[...] = jnp.zeros_like(carry_ref)
    acc_ref[...] = jnp.zeros_like(acc_ref)
    for cp in page_copies(0, 0):
        cp.start()
    new = _pad_rows(new_ref[0], PAGE_SIZE)
    attend(lambda h: new[:, h * HEAD_DIM:(h + 1) * HEAD_DIM],
           lambda h: new[:, SB_W + h * HEAD_DIM:SB_W + (h + 1) * HEAD_DIM], col < tq)

    def body(state):
        n, _ = state
        slot = n % 2
        for cp in page_copies(slot, n):
            cp.wait()

        @pl.when(n + 1 < n_pages)
        def _():
            for cp in page_copies(1 - slot, n + 1):
                cp.start()

        attend(lambda h: buf_ref[slot, h // SUBLANES, :, h % SUBLANES, :],
               lambda h: buf_ref[slot, (SB_HEADS + h) // SUBLANES, :, h % SUBLANES, :], None)
        live = jnp.max(carry_ref[...]) > EXP_UNDERFLOW
        return n + 1, live.astype(jnp.int32)

    n_done, _ = lax.while_loop(lambda st: (st[0] < n_pages) & (st[1] > 0), body, (jnp.int32(0), jnp.int32(1)))

    @pl.when(n_done < n_pages)
    def _():
        for cp in page_copies(n_done % 2, n_done):
            cp.wait()

    zg = z_ref[0]
    for h in range(SB_HEADS):
        sl = slice(h * HEAD_DIM, (h + 1) * HEAD_DIM)
        o_ref[0, :, sl] = (acc_ref[h * t:(h + 1) * t, :] * _silu(zg[:, sl])).astype(o_ref.dtype)


def sb_decode(q, new_kv, cache, page_off, page_table, z):
    b, t, _ = q.shape
    n_pages = page_table.shape[1]
    rows = SB_HEADS * t
    tiles_per_key = 2 * SB_HEADS // SUBLANES
    tok = lambda w: pl.BlockSpec((1, t, w), lambda bi, pt: (bi, 0, 0))
    grid_spec = pltpu.PrefetchScalarGridSpec(
        num_scalar_prefetch=1,
        grid=(b,),
        in_specs=[tok(SB_W), tok(2 * SB_W), tok(SB_W), pl.BlockSpec(memory_space=pl.ANY)],
        out_specs=tok(SB_W),
        scratch_shapes=[
            pltpu.VMEM((2, tiles_per_key, PAGE_SIZE, SUBLANES, HEAD_DIM), F32),
            pltpu.SemaphoreType.DMA((2,)),
            pltpu.VMEM((rows, 1), F32),
            pltpu.VMEM((rows, HEAD_DIM), F32),
        ],
    )
    return pl.pallas_call(
        functools.partial(_sb_decode_kernel, t=t, n_pages=n_pages, page_off=page_off),
        grid_spec=grid_spec,
        out_shape=jax.ShapeDtypeStruct((b, t, SB_W), BF16),
        compiler_params=_cparams(("arbitrary",)),
        name="sb_decode",
    )(page_table, q, new_kv, z, cache)


def _online_update(s, mask, m, l):
    sm = jnp.where(mask, s, NEG)
    m_new = jnp.maximum(m, jnp.max(sm, axis=1, keepdims=True))
    alpha = jnp.exp(m - m_new)
    p = jnp.where(mask, jnp.exp(sm - m_new), 0.0)
    l_new = alpha * l + jnp.sum(p, axis=1, keepdims=True)
    return p, alpha, m_new, l_new


def _flash_step(s, v, m_ref, l_ref, acc_ref):
    m_old = m_ref[...]
    m_new = jnp.maximum(m_old, jnp.max(s, axis=1, keepdims=True))
    alpha = jnp.exp(m_old - m_new)
    p = jnp.exp(s - m_new)
    l_ref[...] = alpha * l_ref[...] + jnp.sum(p, axis=1, keepdims=True)
    m_ref[...] = m_new
    acc_ref[...] = alpha * acc_ref[...] + _dot(p.astype(BF16), v)


def _normalise(acc, l):
    return acc / jnp.maximum(l, 1e-30)


def _diff_lambda(lq1_ref, lk1_ref, lq2_ref, lk2_ref, lambda_init):
    d1 = jnp.sum(lq1_ref[...] * lk1_ref[...], axis=1, keepdims=True)
    d2 = jnp.sum(lq2_ref[...] * lk2_ref[...], axis=1, keepdims=True)
    return jnp.exp(d1) - jnp.exp(d2) + lambda_init


def _diff_finish(o1, o2, lam, hg, zg, lambda_init):
    o = o1 - lam * o2
    ms = jnp.mean(o * o, axis=-1, keepdims=True)
    o = o * lax.rsqrt(ms + EPS) * hg * (1.0 - lambda_init)
    return o * _silu(zg)


def _diff_prompt_kernel(slopes_ref, q_ref, k_ref, v_ref, z_ref, lq1_ref, lk1_ref, lq2_ref, lk2_ref, hg_ref, o_ref,
                        m_ref, l_ref, acc_ref, *, tq, tk, lambda_init):
    h = pl.program_id(1)
    i = pl.program_id(2)
    slope = slopes_ref[h]
    q = q_ref[0] * DIFF_HALF ** -0.5
    qs = [q[:, :DIFF_HALF].astype(BF16), q[:, DIFF_HALF:].astype(BF16)]
    rc = _iota((tq, tk), 0) - _iota((tq, tk), 1)
    bias_rc = slope * rc.astype(F32)
    m_ref[...] = jnp.full_like(m_ref, NEG)
    l_ref[...] = jnp.zeros_like(l_ref)
    acc_ref[...] = jnp.zeros_like(acc_ref)

    def chunk(j, masked):
        start = pl.multiple_of(j * tk, tk)
        k = k_ref[0, pl.ds(start, tk), :]
        v = v_ref[0, pl.ds(start, tk), :].astype(BF16)
        off = i * tq - j * tk
        bias = bias_rc + slope * off.astype(F32)
        for c in range(2):
            s = _dot_nt(qs[c], k[:, c * DIFF_HALF:(c + 1) * DIFF_HALF].astype(BF16)) - bias
            if masked:
                s = jnp.where(rc + off >= 0, s, NEG)
            _flash_step(s, v, m_ref.at[c], l_ref.at[c], acc_ref.at[c])

    n_diag = tq // tk

    def body(j, carry):
        chunk(j, False)
        return carry

    lax.fori_loop(0, i * n_diag, body, 0)
    for d in range(n_diag):
        chunk(i * n_diag + d, True)
    lam = _diff_lambda(lq1_ref, lk1_ref, lq2_ref, lk2_ref, lambda_init)
    out = _diff_finish(_normalise(acc_ref[0], l_ref[0]), _normalise(acc_ref[1], l_ref[1]), lam, hg_ref[...],
                       z_ref[0], lambda_init)
    o_ref[0] = out.astype(o_ref.dtype)


def _smem_spec():
    return pl.BlockSpec(memory_space=pltpu.SMEM)


def diff_prompt(q, kv, z, lq1, lk1, lq2, lk2, head_g, lambda_init, *, tq=256, tk=256):
    b, t, _ = q.shape
    tq = min(tq, t)
    tk = min(tk, tq)
    hq = pl.BlockSpec((1, tq, DIFF_VDIM), lambda bi, h, i: (bi, i, h))
    vec = lambda w: pl.BlockSpec((1, w), lambda bi, h, i: (0, 0))
    return pl.pallas_call(
        functools.partial(_diff_prompt_kernel, tq=tq, tk=tk, lambda_init=lambda_init),
        grid=(b, DIFF_HEADS, t // tq),
        in_specs=[
            _smem_spec(),
            hq,
            pl.BlockSpec((1, t, DIFF_VDIM), lambda bi, h, i: (bi, 0, h)),
            pl.BlockSpec((1, t, DIFF_VDIM), lambda bi, h, i: (bi, 0, DIFF_HEADS + h)),
            hq,
            vec(DIFF_HALF), vec(DIFF_HALF), vec(DIFF_HALF), vec(DIFF_HALF), vec(DIFF_VDIM),
        ],
        out_specs=hq,
        out_shape=jax.ShapeDtypeStruct((b, t, C_W), BF16),
        scratch_shapes=[pltpu.VMEM((2, tq, 1), F32), pltpu.VMEM((2, tq, 1), F32),
                        pltpu.VMEM((2, tq, DIFF_VDIM), F32)],
        compiler_params=_cparams(("parallel", "parallel", "arbitrary")),
        name="diff_prompt",
    )(jnp.asarray(_alibi_slopes(DIFF_HEADS)), q, kv, kv, z,
      lq1.reshape(1, -1), lk1.reshape(1, -1), lq2.reshape(1, -1), lk2.reshape(1, -1), head_g.reshape(1, -1))


def _diff_decode_kernel(pt_ref, q_ref, knew_ref, vnew_ref, kpage_ref, vpage_ref, z_ref, slope_ref, lq1_ref, lk1_ref,
                        lq2_ref, lk2_ref, hg_ref, o_ref, qs_ref, bias_ref, biasn_ref, m_ref, l_ref, acc_ref,
                        *, t, n_pages, lambda_init):
    p = pl.program_id(1)
    scale = DIFF_HALF ** -0.5
    n_tiles = DIFF_HEADS // SUBLANES
    hr = SUBLANES * t
    past = n_pages * PAGE_SIZE

    def tile_bias(a, n_keys, causal):
        shape = (hr, n_keys * SUBLANES)
        r, c = _iota(shape, 0), _iota(shape, 1)
        ok = (r // t) == (c % SUBLANES)
        rel = r % t - c // SUBLANES
        if causal:
            ok = ok & (rel >= 0)
        return jnp.where(ok, -slope_ref[a] * rel.astype(F32), NEG)

    def attend(a, get_k, v, bias):
        s = jnp.concatenate([_dot_nt(qs_ref[c, a], get_k(c).astype(BF16)) + bias for c in range(2)], axis=0)
        m_old = m_ref[a]
        m_new = jnp.maximum(m_old, jnp.max(s, axis=1, keepdims=True))
        alpha = jnp.exp(m_old - m_new)
        pr = jnp.exp(s - m_new)
        l_ref[a] = alpha * l_ref[a] + jnp.sum(pr, axis=1, keepdims=True)
        m_ref[a] = m_new
        acc_ref[a] = alpha * acc_ref[a] + _dot(pr.astype(BF16), v.astype(BF16))

    @pl.when(p == 0)
    def _():
        q = q_ref[0] * scale
        for a in range(n_tiles):
            for c in range(2):
                cols = [(a * SUBLANES + hl) * DIFF_VDIM + c * DIFF_HALF for hl in range(SUBLANES)]
                qs_ref[c, a] = jnp.concatenate([q[:, o:o + DIFF_HALF] for o in cols], axis=0).astype(BF16)
            bias_ref[a] = tile_bias(a, PAGE_SIZE, False)
            biasn_ref[a] = tile_bias(a, t, True)
        m_ref[...] = jnp.full_like(m_ref, NEG)
        l_ref[...] = jnp.zeros_like(l_ref)
        acc_ref[...] = jnp.zeros_like(acc_ref)

    @pl.when(p < n_pages)
    def _():
        base = (past - p * PAGE_SIZE).astype(F32)
        for a in range(n_tiles):
            rows = slice(a * SUBLANES, (a + 1) * SUBLANES)
            get_k = lambda c: kpage_ref[:, rows, c * DIFF_HALF:(c + 1) * DIFF_HALF].reshape(
                PAGE_SIZE * SUBLANES, DIFF_HALF)
            v = vpage_ref[:, rows, :].reshape(PAGE_SIZE * SUBLANES, DIFF_VDIM)
            attend(a, get_k, v, bias_ref[a] - slope_ref[a] * base)

    @pl.when(p == n_pages)
    def _():
        lam = _diff_lambda(lq1_ref, lk1_ref, lq2_ref, lk2_ref, lambda_init)
        zg = z_ref[0]
        hg = hg_ref[...]
        for a in range(n_tiles):
            rows = slice(a * SUBLANES, (a + 1) * SUBLANES)
            get_k = lambda c: knew_ref[0, :, rows, c * DIFF_HALF:(c + 1) * DIFF_HALF].reshape(t * SUBLANES, DIFF_HALF)
            attend(a, get_k, vnew_ref[0, :, rows, :].reshape(t * SUBLANES, DIFF_VDIM), biasn_ref[a])
            o = _normalise(acc_ref[a], l_ref[a])
            for hl in range(SUBLANES):
                r1 = slice(hl * t, (hl + 1) * t)
                r2 = slice(hr + hl * t, hr + (hl + 1) * t)
                sl = slice((a * SUBLANES + hl) * DIFF_VDIM, (a * SUBLANES + hl + 1) * DIFF_VDIM)
                o_ref[0, :, sl] = _diff_finish(o[r1], o[r2], lam, hg, zg[:, sl], lambda_init).astype(o_ref.dtype)


def diff_decode(q, new_kv, cache, page_off, page_table, z, lq1, lk1, lq2, lk2, head_g, lambda_init):
    b, t, _ = q.shape
    n_pages = page_table.shape[1]
    n_tiles = DIFF_HEADS // SUBLANES
    hr = SUBLANES * t
    slopes = np.repeat(_alibi_slopes(DIFF_HEADS), t).reshape(n_tiles, hr, 1)
    tok = lambda w: pl.BlockSpec((1, t, w), lambda bi, p, pt: (bi, 0, 0))
    vec = lambda w: pl.BlockSpec((1, w), lambda bi, p, pt: (0, 0))
    new_spec = lambda kv: pl.BlockSpec((1, t, DIFF_HEADS, DIFF_VDIM), lambda bi, p, pt: (bi, 0, kv, 0))
    page_spec = lambda kv: pl.BlockSpec(
        (None, PAGE_SIZE, DIFF_HEADS, DIFF_VDIM),
        lambda bi, p, pt: (page_off + pt[bi, jnp.minimum(p, n_pages - 1)], 0, kv, 0))
    grid_spec = pltpu.PrefetchScalarGridSpec(
        num_scalar_prefetch=1,
        grid=(b, n_pages + 1),
        in_specs=[
            tok(C_W), new_spec(0), new_spec(1), page_spec(0), page_spec(1), tok(C_W),
            pl.BlockSpec((n_tiles, hr, 1), lambda bi, p, pt: (0, 0, 0)),
            vec(DIFF_HALF), vec(DIFF_HALF), vec(DIFF_HALF), vec(DIFF_HALF), vec(DIFF_VDIM),
        ],
        out_specs=tok(C_W),
        scratch_shapes=[
            pltpu.VMEM((2, n_tiles, hr, DIFF_HALF), BF16),
            pltpu.VMEM((n_tiles, hr, PAGE_SIZE * SUBLANES), F32),
            pltpu.VMEM((n_tiles, hr, t * SUBLANES), F32),
            pltpu.VMEM((n_tiles, 2 * hr, 1), F32),
            pltpu.VMEM((n_tiles, 2 * hr, 1), F32),
            pltpu.VMEM((n_tiles, 2 * hr, DIFF_VDIM), F32),
        ],
    )
    return pl.pallas_call(
        functools.partial(_diff_decode_kernel, t=t, n_pages=n_pages, lambda_init=lambda_init),
        grid_spec=grid_spec,
        out_shape=jax.ShapeDtypeStruct((b, t, C_W), BF16),
        compiler_params=_cparams(("parallel", "arbitrary")),
        name="diff_decode",
    )(page_table, q, new_kv, new_kv, cache, cache, z, jnp.asarray(slopes),
      lq1.reshape(1, -1), lk1.reshape(1, -1), lq2.reshape(1, -1), lk2.reshape(1, -1), head_g.reshape(1, -1))


def _masked_softmax_parts(parts):
    m = None
    for s, mask in parts:
        pm = jnp.max(jnp.where(mask, s, NEG), axis=1, keepdims=True)
        m = pm if m is None else jnp.maximum(m, pm)
    es = [jnp.where(mask, jnp.exp(jnp.where(mask, s, NEG) - m), 0.0) for s, mask in parts]
    den = None
    for e in es:
        d = jnp.sum(e, axis=1, keepdims=True)
        den = d if den is None else den + d
    den = jnp.maximum(den, 1e-30)
    return [e / den for e in es]


def _nsa_prompt_kernel(slopes_ref, q_ref, kc_ref, vc_ref, ks_ref, vs_ref, kw_ref, vw_ref, lg_ref, z_ref,
                       wk_ref, wv_ref, expand_ref, o_ref, kcmp_ref, vcmp_ref, selbias_ref, m_ref, l_ref, acc_ref,
                       *, tq, t_total):
    g = pl.program_id(1)
    i = pl.program_id(2)
    scale = HEAD_DIM ** -0.5
    nb = t_total // SEL_BLOCK
    rows = NSA_GROUP * tq
    tk = tq

    @pl.when(i == 0)
    def _():
        for src, w_ref, dst in ((kc_ref, wk_ref, kcmp_ref), (vc_ref, wv_ref, vcmp_ref)):
            x = src[0].reshape(nb, SEL_BLOCK, HEAD_DIM)
            w = w_ref[0][None]
            even = jnp.sum(x[:, :CMP_BLOCK, :] * w, axis=1)
            odd = jnp.sum(x[:, CMP_BLOCK:, :] * w, axis=1)
            pad = jnp.zeros((LANES - 2 * nb, HEAD_DIM), F32)
            dst[...] = jnp.concatenate([even, odd, pad], axis=0).astype(BF16)

    q = q_ref[0] * scale
    q4 = jnp.concatenate([q[:, zz * HEAD_DIM:(zz + 1) * HEAD_DIM] for zz in range(NSA_GROUP)], axis=0).astype(BF16)
    qpos1 = i * tq + _iota((tq, 1), 0)
    qpos4 = jnp.concatenate([qpos1] * NSA_GROUP, axis=0)
    slope4 = jnp.concatenate(
        [jnp.full((tq, 1), slopes_ref[g * NSA_GROUP + zz], F32) for zz in range(NSA_GROUP)], axis=0)
    col = _iota((1, LANES), 1)

    cidx = jnp.where(col < nb, 2 * col, 2 * (col - nb) + 1)
    dist_c = qpos4 - (cidx * CMP_BLOCK + (CMP_BLOCK - 1))
    s_c = _dot_nt(q4, kcmp_ref[...]) - slope4 * dist_c.astype(F32)
    (p_c,) = _masked_softmax_parts([(s_c, (dist_c >= 0) & (col < 2 * nb))])
    o_cmp = _dot(p_c.astype(BF16), vcmp_ref[...])
    imp = p_c[0:tq]
    for zz in range(1, NSA_GROUP):
        imp = imp + p_c[zz * tq:(zz + 1) * tq]
    pair = imp + pltpu.roll(imp, LANES - nb, 1)

    cur = qpos1 // SEL_BLOCK
    valid = col * SEL_BLOCK <= qpos1
    forced = (col == cur) | (col == 0)
    score = jnp.where(forced, FORCED_SCORE, jnp.where(valid, pair, -1.0))
    score = jnp.where(col < nb, score, -2.0)
    score_t = score.T
    cand = score_t[:nb]
    blk = _iota((nb, 1), 0)
    rank = jnp.zeros((nb, tq), F32)
    for j in range(nb):
        r = score_t[j:j + 1, :]
        ge = jnp.where(r >= cand, 1.0, 0.0)
        gt = jnp.where(r > cand, 1.0, 0.0)
        rank = rank + jnp.where(blk > j, ge, gt)
    sel_t = (rank < float(min(SEL_TOPK, nb))).astype(F32)
    sel = jnp.concatenate([sel_t, jnp.zeros((LANES - nb, tq), F32)], axis=0).T
    selbias_ref[...] = (_dot(sel.astype(BF16), expand_ref[...]) - 1.0) * (-NEG)

    tile = lambda x: jnp.concatenate([x] * NSA_GROUP, axis=0)
    rc4 = tile(_iota((tq, tk), 0) - _iota((tq, tk), 1))
    bias_rc = slope4 * rc4.astype(F32)

    def reset():
        m_ref[...] = jnp.full_like(m_ref, NEG)
        l_ref[...] = jnp.zeros_like(l_ref)
        acc_ref[...] = jnp.zeros_like(acc_ref)

    def step(k_ref, v_ref, j, admit):
        start = pl.multiple_of(j * tk, tk)
        k = k_ref[0, pl.ds(start, tk), :].astype(BF16)
        v = v_ref[0, pl.ds(start, tk), :].astype(BF16)
        off = (i - j) * tk
        s = _dot_nt(q4, k) - (bias_rc + slope4 * off.astype(F32))
        _flash_step(admit(s, start, rc4 + off), v, m_ref, l_ref, acc_ref)

    def loop(n, fn):
        def body(c, carry):
            fn(c)
            return carry
        lax.fori_loop(0, n, body, 0)

    chosen = lambda s, start: s + tile(selbias_ref[:, pl.ds(start, tk)])
    reset()
    loop(i, lambda j: step(ks_ref, vs_ref, j, lambda s, start, dist: chosen(s, start)))
    step(ks_ref, vs_ref, i, lambda s, start, dist: jnp.where(dist >= 0, chosen(s, start), NEG))
    o_slc = _normalise(acc_ref[...], l_ref[...])

    n_inside = WINDOW // tk - 1
    reset()
    step(kw_ref, vw_ref, i, lambda s, start, dist: jnp.where(dist >= 0, s, NEG))
    loop(jnp.minimum(i, n_inside), lambda c: step(kw_ref, vw_ref, i - 1 - c, lambda s, start, dist: s))

    @pl.when(i > n_inside)
    def _():
        step(kw_ref, vw_ref, i - 1 - n_inside, lambda s, start, dist: jnp.where(dist < WINDOW, s, NEG))

    o_win = _normalise(acc_ref[...], l_ref[...])

    gates = pltpu.roll(_sigmoid(lg_ref[0]), (LANES - 3 * NSA_GROUP * g) % LANES, 1)
    zg = z_ref[0]
    for zz in range(NSA_GROUP):
        r = slice(zz * tq, (zz + 1) * tq)
        mix = (gates[:, 3 * zz:3 * zz + 1] * o_cmp[r] + gates[:, 3 * zz + 1:3 * zz + 2] * o_slc[r]
               + gates[:, 3 * zz + 2:3 * zz + 3] * o_win[r])
        sl = slice(zz * HEAD_DIM, (zz + 1) * HEAD_DIM)
        o_ref[0, :, sl] = (mix * _silu(zg[:, sl])).astype(o_ref.dtype)


def _cmp_weight_rows(w):
    return jnp.broadcast_to(w.T[:, :, None], (NSA_KV_HEADS, CMP_BLOCK, HEAD_DIM)).astype(F32)


def nsa_prompt(q, nskv, win, logits, z, cmp_wk, cmp_wv, *, tq=128):
    b, t, _ = q.shape
    nb = t // SEL_BLOCK
    assert t % tq == 0 and 2 * nb <= LANES and WINDOW % tq == 0
    gw = NSA_GROUP * HEAD_DIM
    expand = (np.arange(LANES)[:, None] == (np.arange(t)[None, :] // SEL_BLOCK)).astype(np.float32)
    kvspec = lambda kind: pl.BlockSpec((1, t, HEAD_DIM), lambda bi, g, i: (bi, 0, kind * NSA_KV_HEADS + g))
    qspec = pl.BlockSpec((1, tq, gw), lambda bi, g, i: (bi, i, g))
    wspec = pl.BlockSpec((1, CMP_BLOCK, HEAD_DIM), lambda bi, g, i: (g, 0, 0))
    return pl.pallas_call(
        functools.partial(_nsa_prompt_kernel, tq=tq, t_total=t),
        grid=(b, NSA_KV_HEADS, t // tq),
        in_specs=[
            _smem_spec(),
            qspec,
            kvspec(0), kvspec(1), kvspec(2), kvspec(3),
            kvspec(0), kvspec(1),
            pl.BlockSpec((1, tq, LANES), lambda bi, g, i: (bi, i, 0)),
            qspec,
            wspec, wspec,
            pl.BlockSpec((LANES, t), lambda bi, g, i: (0, 0)),
        ],
        out_specs=qspec,
        out_shape=jax.ShapeDtypeStruct((b, t, NSA_W), BF16),
        scratch_shapes=[
            pltpu.VMEM((LANES, HEAD_DIM), BF16),
            pltpu.VMEM((LANES, HEAD_DIM), BF16),
            pltpu.VMEM((tq, t), F32),
            pltpu.VMEM((NSA_GROUP * tq, 1), F32),
            pltpu.VMEM((NSA_GROUP * tq, 1), F32),
            pltpu.VMEM((NSA_GROUP * tq, HEAD_DIM), F32),
        ],
        compiler_params=_cparams(("parallel", "parallel", "arbitrary")),
        name="nsa_prompt",
    )(jnp.asarray(_alibi_slopes(NSA_HEADS)), q, nskv, nskv, nskv, nskv, win, win, logits, z,
      _cmp_weight_rows(cmp_wk), _cmp_weight_rows(cmp_wv), jnp.asarray(expand, BF16))


def _nsa_compress_kernel(pt_ref, page_ref, w_ref, even_ref, odd_ref):
    n_cmp = PAGE_SIZE // CMP_BLOCK
    x = page_ref[...].reshape(n_cmp, CMP_BLOCK, 2 * NSA_KV_HEADS, HEAD_DIM)
    c = jnp.sum(x * w_ref[...][None], axis=1)
    even_ref[0] = jnp.concatenate([c[r:r + 1] for r in range(0, n_cmp, 2)], axis=0)
    odd_ref[0] = jnp.concatenate([c[r:r + 1] for r in range(1, n_cmp, 2)], axis=0)


def nsa_compress_pages(cache, page_off, page_table, cmp_wk, cmp_wv):
    b, n_pages = page_table.shape
    per_page = PAGE_SIZE // CMP_BLOCK // 2
    w = jnp.concatenate([cmp_wk, cmp_wv], axis=1).astype(F32)
    w = jnp.broadcast_to(w[:, :, None], (CMP_BLOCK, 2 * NSA_KV_HEADS, HEAD_DIM))
    cmp_spec = pl.BlockSpec((1, per_page, 2 * NSA_KV_HEADS, HEAD_DIM), lambda bi, p, pt: (bi, p, 0, 0))
    grid_spec = pltpu.PrefetchScalarGridSpec(
        num_scalar_prefetch=1,
        grid=(b, n_pages),
        in_specs=[
            pl.BlockSpec((None, PAGE_SIZE, 2 * NSA_KV_HEADS, HEAD_DIM),
                         lambda bi, p, pt: (page_off + pt[bi, p], 0, 0, 0)),
            pl.BlockSpec((CMP_BLOCK, 2 * NSA_KV_HEADS, HEAD_DIM), lambda bi, p, pt: (0, 0, 0)),
        ],
        out_specs=[cmp_spec, cmp_spec],
    )
    return pl.pallas_call(
        _nsa_compress_kernel,
        grid_spec=grid_spec,
        out_shape=[jax.ShapeDtypeStruct((b, n_pages * per_page, 2 * NSA_KV_HEADS, HEAD_DIM), F32)] * 2,
        compiler_params=_cparams(("parallel", "parallel")),
        name="nsa_compress_pages",
    )(page_table, cache, w)


def _nsa_decode_kernel(pt_ref, q_ref, even_ref, odd_ref, page_ref, new_ref, wst_ref, wnew_ref, lg_ref, z_ref, slope_ref,
                       o_ref, qg_ref, sel_ref, ocmp_ref, m_ref, l_ref, acc_ref, *, t, n_pages):
    p = pl.program_id(1)
    scale = HEAD_DIM ** -0.5
    rows = NSA_HEADS * t
    grp_rows = NSA_GROUP * t
    past = n_pages * PAGE_SIZE
    n_blk = past // SEL_BLOCK
    half = n_blk
    tq = _iota((rows, 1), 0) % t
    qpos = past + tq
    slope = slope_ref[...]
    col = _iota((1, PAGE_SIZE), 1)

    def scores(get_k):
        return jnp.concatenate([_dot_nt(qg_ref[gg], get_k(gg).astype(BF16)) for gg in range(NSA_KV_HEADS)],
                               axis=0) * scale

    def group_pv(pr, get_v):
        return jnp.concatenate([_dot(pr[gg * grp_rows:(gg + 1) * grp_rows], get_v(gg).astype(BF16))
                                for gg in range(NSA_KV_HEADS)], axis=0)

    def rows_kv(ref3):
        return (lambda gg: ref3[:, gg, :]), (lambda gg: ref3[:, NSA_KV_HEADS + gg, :])

    def lanes_kv(x, first_block):
        blk = lambda i: x[:, (first_block + i) * HEAD_DIM:(first_block + i + 1) * HEAD_DIM]
        return (lambda gg: blk(gg)), (lambda gg: blk(NSA_KV_HEADS + gg))

    @pl.when(p == 0)
    def _():
        q = q_ref[0]
        for gg in range(NSA_KV_HEADS):
            heads = range(gg * NSA_GROUP, (gg + 1) * NSA_GROUP)
            qg_ref[gg] = jnp.concatenate([q[:, h * HEAD_DIM:(h + 1) * HEAD_DIM] for h in heads], axis=0).astype(BF16)
        parts, vals = [], []
        for par, cmp_ref in enumerate((even_ref, odd_ref)):
            get_k, get_v = rows_kv(cmp_ref.at[0])
            vals.append(get_v)
            c_end = (2 * _iota((1, half), 1) + par) * CMP_BLOCK + (CMP_BLOCK - 1)
            dist = qpos - c_end
            parts.append((scores(get_k) - slope * dist.astype(F32), dist >= 0))
        p_e, p_o = _masked_softmax_parts(parts)
        ocmp_ref[...] = group_pv(p_e.astype(BF16), vals[0]) + group_pv(p_o.astype(BF16), vals[1])
        pe = p_e + p_o
        blkcol = _iota((1, n_blk), 1)
        picked = []
        for gg in range(NSA_KV_HEADS):
            imp = pe[gg * grp_rows:gg * grp_rows + t]
            for zz in range(1, NSA_GROUP):
                imp = imp + pe[gg * grp_rows + zz * t:gg * grp_rows + (zz + 1) * t]
            sc = jnp.where(blkcol == 0, -1.0, imp)
            sel = (blkcol == 0)
            for _ in range(min(SEL_TOPK, n_blk + 1) - 2):
                mx = jnp.max(sc, axis=1, keepdims=True)
                first = jnp.min(jnp.where(sc == mx, blkcol, n_blk), axis=1, keepdims=True)
                hit = blkcol == first
                sel = sel | hit
                sc = jnp.where(hit, -1.0, sc)
            self32 = sel.astype(F32)
            picked.extend([self32] * NSA_GROUP)
        sel_ref[...] = jnp.concatenate(picked, axis=0).astype(BF16)
        m_ref[...] = jnp.full_like(m_ref, NEG)
        l_ref[...] = jnp.zeros_like(l_ref)
        acc_ref[...] = jnp.zeros_like(acc_ref)

    def attend(get_k, get_v, dist, mask):
        s = scores(get_k) - slope * dist.astype(F32)
        pr, alpha, m_new, l_new = _online_update(s, mask, m_ref[...], l_ref[...])
        m_ref[...] = m_new
        l_ref[...] = l_new
        acc_ref[...] = alpha * acc_ref[...] + group_pv(pr.astype(BF16), get_v)

    @pl.when(p < n_pages)
    def _():
        per_page = PAGE_SIZE // SEL_BLOCK
        expand = (_iota((n_blk, PAGE_SIZE), 0) == per_page * p + _iota((n_blk, PAGE_SIZE), 1) // SEL_BLOCK)
        chosen = _dot(sel_ref[...], expand.astype(BF16)) > 0.5
        dist = qpos - (p * PAGE_SIZE + col)
        get_k, get_v = rows_kv(page_ref)
        attend(get_k, get_v, dist, chosen & (dist >= 0))

    @pl.when(p == n_pages)
    def _():
        new = _pad_rows(new_ref[0], PAGE_SIZE)
        dist = qpos - (past + col)
        get_k, get_v = lanes_kv(new, 2 * NSA_KV_HEADS)
        attend(get_k, get_v, dist, (dist >= 0) & (col < t))
        o_slc = _normalise(acc_ref[...], l_ref[...])
        n_state = wst_ref.shape[1]
        dist_s = qpos - (past - n_state + _iota((1, n_state), 1))
        ks_state, vs_state = rows_kv(wst_ref.at[0])
        ks_new, vs_new = lanes_kv(_pad_rows(wnew_ref[0], PAGE_SIZE), 0)
        p_s, p_n = _masked_softmax_parts([
            (scores(ks_state) - slope * dist_s.astype(F32), (dist_s >= 0) & (dist_s < WINDOW)),
            (scores(ks_new) - slope * dist.astype(F32), (dist >= 0) & (dist < WINDOW) & (col < t)),
        ])
        o_win = group_pv(p_s.astype(BF16), vs_state) + group_pv(p_n.astype(BF16), vs_new)
        o_cmp = ocmp_ref[...]
        gates = _sigmoid(lg_ref[0])
        zg = z_ref[0]
        for h in range(NSA_HEADS):
            r = slice(h * t, (h + 1) * t)
            mix = (gates[:, 3 * h:3 * h + 1] * o_cmp[r] + gates[:, 3 * h + 1:3 * h + 2] * o_slc[r]
                   + gates[:, 3 * h + 2:3 * h + 3] * o_win[r])
            sl = slice(h * HEAD_DIM, (h + 1) * HEAD_DIM)
            o_ref[0, :, sl] = (mix * _silu(zg[:, sl])).astype(o_ref.dtype)


def nsa_decode(q, cmp_even, cmp_odd, cache, page_off, page_table, new_nskv, win_state, win_new, logits, z):
    b, t, _ = q.shape
    n_pages = page_table.shape[1]
    past = n_pages * PAGE_SIZE
    n_blk = past // SEL_BLOCK
    n_state = win_state.shape[1]
    assert past % SEL_BLOCK == 0 and t < CMP_BLOCK and n_blk + 1 >= SEL_TOPK and n_state >= WINDOW - 1
    rows = NSA_HEADS * t
    slopes = np.repeat(_alibi_slopes(NSA_HEADS), t).reshape(rows, 1)
    tok = lambda w: pl.BlockSpec((1, t, w), lambda bi, p, pt: (bi, 0, 0))
    grid_spec = pltpu.PrefetchScalarGridSpec(
        num_scalar_prefetch=1,
        grid=(b, n_pages + 1),
        in_specs=[
            tok(NSA_W),
            pl.BlockSpec((1, n_blk, 2 * NSA_KV_HEADS, HEAD_DIM), lambda bi, p, pt: (bi, 0, 0, 0)),
            pl.BlockSpec((1, n_blk, 2 * NSA_KV_HEADS, HEAD_DIM), lambda bi, p, pt: (bi, 0, 0, 0)),
            pl.BlockSpec((None, PAGE_SIZE, 2 * NSA_KV_HEADS, HEAD_DIM),
                         lambda bi, p, pt: (page_off + pt[bi, jnp.minimum(p, n_pages - 1)], 0, 1, 0)),
            tok(4 * NSA_KV_W),
            pl.BlockSpec((1, n_state, 2 * NSA_KV_HEADS, HEAD_DIM), lambda bi, p, pt: (bi, 0, 0, 0)),
            tok(2 * NSA_KV_W),
            tok(LANES),
            tok(NSA_W),
            pl.BlockSpec((rows, 1), lambda bi, p, pt: (0, 0)),
        ],
        out_specs=tok(NSA_W),
        scratch_shapes=[
            pltpu.VMEM((NSA_KV_HEADS, NSA_GROUP * t, HEAD_DIM), BF16),
            pltpu.VMEM((rows, n_blk), BF16),
            pltpu.VMEM((rows, HEAD_DIM), F32),
            pltpu.VMEM((rows, 1), F32),
            pltpu.VMEM((rows, 1), F32),
            pltpu.VMEM((rows, HEAD_DIM), F32),
        ],
    )
    return pl.pallas_call(
        functools.partial(_nsa_decode_kernel, t=t, n_pages=n_pages),
        grid_spec=grid_spec,
        out_shape=jax.ShapeDtypeStruct((b, t, NSA_W), BF16),
        compiler_params=_cparams(("parallel", "arbitrary")),
        name="nsa_decode",
    )(page_table, q, cmp_even, cmp_odd, cache, new_nskv, win_state, win_new, logits, z, jnp.asarray(slopes))


def _ab_weights(w_in):
    o = np.cumsum([0, SB_W, 2 * SB_W, SB_W, NSA_W, 4 * NSA_KV_W, 2 * NSA_KV_W, 3 * NSA_HEADS, NSA_W])
    cut = lambda a, b_: w_in[:, a:b_].astype(BF16)
    logits = jnp.pad(w_in[:, o[6]:o[7]], ((0, 0), (0, LANES - 3 * NSA_HEADS))).astype(BF16)
    return dict(sbq=cut(o[0], o[1]), sbkv=cut(o[1], o[2]), sbz=cut(o[2], o[3]), nsq=cut(o[3], o[4]),
                nskv=cut(o[4], o[5]), win=cut(o[5], o[6]), logits=logits, nsz=cut(o[7], o[8]))


def _c_weights(w_in):
    cut = lambda a, b_: w_in[:, a:b_].astype(BF16)
    return dict(q=cut(0, C_W), kv=cut(C_W, 3 * C_W), z=cut(3 * C_W, 4 * C_W))


def kernel(x_prompt, x_sample, p_prompt, p_sample, cache_sb_kv, cache_nsa_kv, state_nsa_win_kv, cache_diff_kv,
           page_table, norm_g, w_in_ab, nsa_cmp_wk, nsa_cmp_wv, w_out_ab, w_in_c, diff_lq1, diff_lk1, diff_lq2,
           diff_lk2, diff_head_g, w_out_c, ple_norm_g, w_ple_gate, w_ple_proj, final_norm_g):
    bp, tp, d = x_prompt.shape
    bs, ts, _ = x_sample.shape
    depth = norm_g.shape[0]
    n_phys = cache_sb_kv.shape[1]

    def run(x, p_emb, sample):
        b, t, _ = x.shape
        h = x.reshape(b * t, d)
        sb_rows, nsa_rows, win_rows, diff_rows = [], [], [], []
        for i in range(depth):
            j = i // 2
            hn = rmsnorm(h, norm_g[i], BF16)
            if i % 2 == 0:
                w = _ab_weights(w_in_ab[j])
                proj = {name: matmul(hn, wm).reshape(b, t, -1) for name, wm in w.items()}
                win_new = proj["win"].reshape(b, t, 2, NSA_KV_HEADS, HEAD_DIM)
                if sample:
                    sb_mixed = sb_decode(proj["sbq"], proj["sbkv"],
                                         cache_sb_kv.reshape(-1, PAGE_SIZE, 2 * SB_HEADS, HEAD_DIM), j * n_phys,
                                         page_table, proj["sbz"])
                    nsa_cache = cache_nsa_kv.reshape(-1, PAGE_SIZE, 4 * NSA_KV_HEADS, HEAD_DIM)
                    cmp_even, cmp_odd = nsa_compress_pages(nsa_cache, j * n_phys, page_table, nsa_cmp_wk[j],
                                                           nsa_cmp_wv[j])
                    win_rows_in = state_nsa_win_kv[j].reshape(b, -1, 2 * NSA_KV_HEADS, HEAD_DIM)
                    ns_mixed = nsa_decode(proj["nsq"], cmp_even, cmp_odd, nsa_cache, j * n_phys, page_table,
                                          proj["nskv"], win_rows_in, proj["win"], proj["logits"], proj["nsz"])
                    win_all = jnp.concatenate([state_nsa_win_kv[j], win_new], axis=1)
                else:
                    sb_mixed = sb_prompt(proj["sbq"], proj["sbkv"], proj["sbz"])
                    ns_mixed = nsa_prompt(proj["nsq"], proj["nskv"], proj["win"], proj["logits"], proj["nsz"],
                                          nsa_cmp_wk[j], nsa_cmp_wv[j])
                    win_all = win_new
                mixed = jnp.concatenate([sb_mixed, ns_mixed], axis=-1).reshape(b * t, -1)
                h = matmul(mixed, w_out_ab[j].astype(BF16), residual=h)
                sb_rows.append(proj["sbkv"].reshape(b, t, 2, SB_HEADS, HEAD_DIM))
                nsa_rows.append(proj["nskv"].reshape(b, t, 4, NSA_KV_HEADS, HEAD_DIM))
                keep = min(WINDOW, win_all.shape[1])
                win_rows.append(win_all[:, win_all.shape[1] - keep:])
            else:
                lambda_init = 0.8 - 0.6 * math.exp(-0.3 * i)
                w = _c_weights(w_in_c[j])
                proj = {name: matmul(hn, wm).reshape(b, t, -1) for name, wm in w.items()}
                lam_args = (diff_lq1[j], diff_lk1[j], diff_lq2[j], diff_lk2[j], diff_head_g[j], lambda_init)
                if sample:
                    mixed = diff_decode(proj["q"], proj["kv"].reshape(b, t, 2 * DIFF_HEADS, DIFF_VDIM),
                                        cache_diff_kv.reshape(-1, PAGE_SIZE, 2 * DIFF_HEADS, DIFF_VDIM), j * n_phys,
                                        page_table, proj["z"], *lam_args)
                else:
                    mixed = diff_prompt(proj["q"], proj["kv"], proj["z"], *lam_args)
                h = matmul(mixed.reshape(b * t, -1), w_out_c[j].astype(BF16), residual=h)
                diff_rows.append(proj["kv"].reshape(b, t, 2, DIFF_HEADS, DIFF_VDIM))
            hn2 = rmsnorm(h, ple_norm_g[i], BF16)
            h = ple(hn2, w_ple_gate[i].astype(BF16), p_emb[i].reshape(b * t, -1).astype(BF16),
                    w_ple_proj[i].astype(BF16), h)
        y = rmsnorm(h, final_norm_g, F32).reshape(b, t, d)
        return y, jnp.stack(sb_rows), jnp.stack(nsa_rows), jnp.stack(win_rows), jnp.stack(diff_rows)

    y_p, sb_p, nsa_p, win_p, diff_p = run(x_prompt, p_prompt, False)
    y_s, sb_s, nsa_s, win_s, diff_s = run(x_sample, p_sample, True)
    return (y_p, y_s, sb_p, sb_s, nsa_p, nsa_s, win_p, win_s, diff_p, diff_s)
```

```python
import functools
import math

import jax
import jax.numpy as jnp
import numpy as np
from jax import lax
from jax.experimental import pallas as pl
from jax.experimental.pallas import tpu as pltpu

F32 = jnp.float32
BF16 = jnp.bfloat16

HEAD_DIM = 128
SB_HEADS = 16
NSA_HEADS = 16
NSA_KV_HEADS = 4
NSA_GROUP = NSA_HEADS // NSA_KV_HEADS
CMP_BLOCK = 32
SEL_BLOCK = 64
SEL_TOPK = 16
WINDOW = 512
DIFF_HEADS = 16
DIFF_HALF = 128
DIFF_VDIM = 2 * DIFF_HALF
PAGE_SIZE = 128
EPS = 1e-6
NEG = -1e30
FORCED_SCORE = 1e4

SB_W = SB_HEADS * HEAD_DIM
NSA_W = NSA_HEADS * HEAD_DIM
NSA_KV_W = NSA_KV_HEADS * HEAD_DIM
C_W = DIFF_HEADS * DIFF_VDIM
LANES = 128
SUBLANES = 8
EXP_UNDERFLOW = -110.0
VMEM_LIMIT = 56 * 1024 * 1024


def _alibi_slopes(n):
    return np.asarray(2.0 ** (-8.0 * np.arange(1, n + 1) / n), dtype=np.float32)


def _cparams(sem):
    return pltpu.CompilerParams(dimension_semantics=sem, vmem_limit_bytes=VMEM_LIMIT)


def _dot(a, b):
    return jnp.dot(a, b, preferred_element_type=F32)


def _dot_nt(a, b):
    return lax.dot_general(a, b, (((1,), (1,)), ((), ())), preferred_element_type=F32)


def _silu(z):
    return z * (1.0 / (1.0 + jnp.exp(-z)))


def _sigmoid(z):
    return 1.0 / (1.0 + jnp.exp(-z))


def _iota(shape, dim):
    return lax.broadcasted_iota(jnp.int32, shape, dim)


def _rmsnorm_kernel(x_ref, g_ref, o_ref):
    x = x_ref[...]
    ms = jnp.mean(x * x, axis=-1, keepdims=True)
    o_ref[...] = (x * lax.rsqrt(ms + EPS) * g_ref[...]).astype(o_ref.dtype)


def rmsnorm(x, g, out_dtype):
    m, d = x.shape
    tm = min(m, 256)
    return pl.pallas_call(
        _rmsnorm_kernel,
        grid=(m // tm,),
        in_specs=[pl.BlockSpec((tm, d), lambda i: (i, 0)), pl.BlockSpec((1, d), lambda i: (0, 0))],
        out_specs=pl.BlockSpec((tm, d), lambda i: (i, 0)),
        out_shape=jax.ShapeDtypeStruct((m, d), out_dtype),
        compiler_params=_cparams(("parallel",)),
        name="rmsnorm",
    )(x, g.reshape(1, d))


def _mm_kernel(*refs, n_parts, has_residual):
    a_refs, w_refs = refs[:n_parts], refs[n_parts:2 * n_parts]
    r_ref = refs[2 * n_parts] if has_residual else None
    o_ref = refs[2 * n_parts + has_residual]
    wb_refs = refs[2 * n_parts + has_residual + 1:]

    @pl.when(pl.program_id(1) == 0)
    def _():
        for w_ref, wb_ref in zip(w_refs, wb_refs):
            wb_ref[...] = w_ref[...].astype(BF16)

    acc = _dot(a_refs[0][...], wb_refs[0][...])
    for a_ref, wb_ref in zip(a_refs[1:], wb_refs[1:]):
        acc = acc + _dot(a_ref[...], wb_ref[...])
    if has_residual:
        acc = r_ref[...] + acc
    o_ref[...] = acc


def matmul(a_parts, w, *, col0=0, n=None, residual=None):
    m, kp = a_parts[0].shape
    n_parts = len(a_parts)
    assert all(a.shape == (m, kp) for a in a_parts) and w.shape[0] == n_parts * kp
    n = w.shape[1] - col0 if n is None else n
    tm = min(m, 1024)
    tn = min(n, 512)
    assert m % tm == 0 and n % tn == 0 and col0 % tn == 0
    in_specs = [pl.BlockSpec((tm, kp), lambda j, i: (i, 0)) for _ in a_parts]
    in_specs += [pl.BlockSpec((kp, tn), lambda j, i, part=part: (part, col0 // tn + j)) for part in range(n_parts)]
    args = list(a_parts) + [w] * n_parts
    if residual is not None:
        in_specs.append(pl.BlockSpec((tm, tn), lambda j, i: (i, j)))
        args.append(residual)
    return pl.pallas_call(
        functools.partial(_mm_kernel, n_parts=n_parts, has_residual=residual is not None),
        grid=(n // tn, m // tm),
        in_specs=in_specs,
        out_specs=pl.BlockSpec((tm, tn), lambda j, i: (i, j)),
        out_shape=jax.ShapeDtypeStruct((m, n), F32),
        scratch_shapes=[pltpu.VMEM((kp, tn), BF16) for _ in a_parts],
        compiler_params=_cparams(("parallel", "arbitrary")),
        name="matmul",
    )(*args)


def _ple_kernel(hn_ref, wg_ref, p_ref, wp_ref, h_ref, o_ref, wgb_ref):
    @pl.when(pl.program_id(1) == 0)
    def _():
        wgb_ref[...] = wg_ref[...].astype(BF16)

    gate = _sigmoid(_dot(hn_ref[...], wgb_ref[...]))
    proj = _dot(p_ref[...], wp_ref[...].astype(BF16))
    o_ref[...] = h_ref[...] + gate * proj


def ple(hn, wg, p, wp, h):
    m, d = hn.shape
    n = wg.shape[1]
    pd = p.shape[1]
    tm = min(m, 1024)
    tn = min(n, 512)
    return pl.pallas_call(
        _ple_kernel,
        grid=(n // tn, m // tm),
        in_specs=[
            pl.BlockSpec((tm, d), lambda j, i: (i, 0)),
            pl.BlockSpec((d, tn), lambda j, i: (0, j)),
            pl.BlockSpec((tm, pd), lambda j, i: (i, 0)),
            pl.BlockSpec((pd, tn), lambda j, i: (0, j)),
            pl.BlockSpec((tm, tn), lambda j, i: (i, j)),
        ],
        out_specs=pl.BlockSpec((tm, tn), lambda j, i: (i, j)),
        out_shape=jax.ShapeDtypeStruct((m, n), F32),
        scratch_shapes=[pltpu.VMEM((d, tn), BF16)],
        compiler_params=_cparams(("parallel", "arbitrary")),
        name="ple",
    )(hn, wg, p, wp, h)


def _suffix_sum_matrix(c):
    return (_iota((c, c), 0) >= _iota((c, c), 1)).astype(BF16)


def _sb_prompt_kernel(q_ref, k_ref, v_ref, z_ref, o_ref, carry_ref, acc_ref, *, tq, tk):
    i = pl.program_id(2)
    qb = (q_ref[0] * HEAD_DIM ** -0.5).astype(BF16)
    incl_mat = _suffix_sum_matrix(tk)
    rc = _iota((tq, tk), 0) - _iota((tq, tk), 1)
    carry_ref[...] = jnp.zeros_like(carry_ref)
    acc_ref[...] = jnp.zeros_like(acc_ref)

    def chunk(j, masked):
        start = pl.multiple_of(j * tk, tk)
        k = k_ref[0, pl.ds(start, tk), :].astype(BF16)
        v = v_ref[0, pl.ds(start, tk), :].astype(BF16)
        z = _dot_nt(qb, k)
        lb = jnp.minimum(z, 0.0) - jnp.log(1.0 + jnp.exp(-jnp.abs(z)))
        lk = lb - z
        if masked:
            mask = rc > j * tk - i * tq
            lb = jnp.where(mask, lb, 0.0)
            lk = jnp.where(mask, lk, 0.0)
        hi = lk.astype(BF16)
        lo = (lk - hi.astype(F32)).astype(BF16)
        incl = _dot(hi, incl_mat) + _dot(lo, incl_mat)
        carry = carry_ref[...]
        a = jnp.exp(lb + (incl - lk + carry))
        if masked:
            a = jnp.where(mask, a, 0.0)
        acc_ref[...] += _dot(a.astype(BF16), v)
        carry_ref[...] = carry + incl[:, 0:1]

    n_diag = tq // tk
    top = (i + 1) * n_diag - 1
    for d in range(n_diag):
        chunk(top - d, True)

    def body(state):
        j, _ = state
        chunk(j, False)
        live = jnp.max(carry_ref[...]) > EXP_UNDERFLOW
        return j - 1, live.astype(jnp.int32)

    lax.while_loop(lambda st: (st[0] >= 0) & (st[1] > 0), body, (top - n_diag, jnp.int32(1)))
    o_ref[0] = (acc_ref[...] * _silu(z_ref[0])).astype(o_ref.dtype)


def sb_prompt(q, kv, z, *, tq=512, tk=256):
    b, t, _ = q.shape
    tq = min(tq, t)
    tk = min(tk, tq)
    hq = pl.BlockSpec((1, tq, HEAD_DIM), lambda bi, h, i: (bi, i, h))
    return pl.pallas_call(
        functools.partial(_sb_prompt_kernel, tq=tq, tk=tk),
        grid=(b, SB_HEADS, t // tq),
        in_specs=[
            hq,
            pl.BlockSpec((1, t, HEAD_DIM), lambda bi, h, i: (bi, 0, h)),
            pl.BlockSpec((1, t, HEAD_DIM), lambda bi, h, i: (bi, 0, SB_HEADS + h)),
            hq,
        ],
        out_specs=hq,
        out_shape=jax.ShapeDtypeStruct((b, t, SB_W), BF16),
        scratch_shapes=[pltpu.VMEM((tq, 1), F32), pltpu.VMEM((tq, HEAD_DIM), F32)],
        compiler_params=_cparams(("parallel", "parallel", "arbitrary")),
        name="sb_prompt",
    )(q, kv, kv, z)


def _pad_rows(x, rows):
    return jnp.concatenate([x, jnp.zeros((rows - x.shape[0], x.shape[1]), x.dtype)], axis=0)


def _sb_decode_kernel(pt_ref, q_ref, new_ref, z_ref, cache_ref, o_ref, buf_ref, sem_ref, carry_ref, acc_ref,
                      *, t, n_pages, page_off):
    b = pl.program_id(0)
    scale = HEAD_DIM ** -0.5
    rows = SB_HEADS * t
    tiles_per_key = 2 * SB_HEADS // SUBLANES
    tq = _iota((rows, 1), 0) % t
    col = _iota((1, PAGE_SIZE), 1)
    qb = q_ref[0].astype(BF16)

    def page_copies(slot, n):
        page = page_off + pt_ref[b, n_pages - 1 - n]
        return [pltpu.make_async_copy(cache_ref.at[page, :, pl.ds(r * SUBLANES, SUBLANES), :],
                                      buf_ref.at[slot, r], sem_ref.at[slot]) for r in range(tiles_per_key)]

    def attend(get_k, get_v, mask):
        z = jnp.concatenate([_dot_nt(qb[:, h * HEAD_DIM:(h + 1) * HEAD_DIM], get_k(h).astype(BF16))
                             for h in range(SB_HEADS)], axis=0) * scale
        carry = carry_ref[...]
        lb = jnp.minimum(z, 0.0) - jnp.log(1.0 + jnp.exp(-jnp.abs(z)))
        lk = lb - z
        if mask is not None:
            lb = jnp.where(mask, lb, 0.0)
            lk = jnp.where(mask, lk, 0.0)
        incl_mat = _suffix_sum_matrix(PAGE_SIZE)
        hi = lk.astype(BF16)
        lo = (lk - hi.astype(F32)).astype(BF16)
        incl = _dot(hi, incl_mat) + _dot(lo, incl_mat)
        a = jnp.exp(lb + (incl - lk + carry))
        if mask is not None:
            a = jnp.where(mask, a, 0.0)
        a = a.astype(BF16)
        for h in range(SB_HEADS):
            acc_ref[h * t:(h + 1) * t, :] += _dot(a[h * t:(h + 1) * t, :], get_v(h).astype(BF16))
        carry_ref[...] = carry + incl[:, 0:1]

    carry_ref[...] = jnp.zeros_like(carry_ref)
    acc_ref[...] = jnp.zeros_like(acc_ref)
    for cp in page_copies(0, 0):
        cp.start()
    new = _pad_rows(new_ref[0], PAGE_SIZE)
    attend(lambda h: new[:, h * HEAD_DIM:(h + 1) * HEAD_DIM],
           lambda h: new[:, SB_W + h * HEAD_DIM:SB_W + (h + 1) * HEAD_DIM], col < tq)

    def body(state):
        n, _ = state
        slot = n % 2
        for cp in page_copies(slot, n):
            cp.wait()

        @pl.when(n + 1 < n_pages)
        def _():
            for cp in page_copies(1 - slot, n + 1):
                cp.start()

        attend(lambda h: buf_ref[slot, h // SUBLANES, :, h % SUBLANES, :],
               lambda h: buf_ref[slot, (SB_HEADS + h) // SUBLANES, :, h % SUBLANES, :], None)
        live = jnp.max(carry_ref[...]) > EXP_UNDERFLOW
        return n + 1, live.astype(jnp.int32)

    n_done, _ = lax.while_loop(lambda st: (st[0] < n_pages) & (st[1] > 0), body, (jnp.int32(0), jnp.int32(1)))

    @pl.when(n_done < n_pages)
    def _():
        for cp in page_copies(n_done % 2, n_done):
            cp.wait()

    zg = z_ref[0]
    for h in range(SB_HEADS):
        sl = slice(h * HEAD_DIM, (h + 1) * HEAD_DIM)
        o_ref[0, :, sl] = (acc_ref[h * t:(h + 1) * t, :] * _silu(zg[:, sl])).astype(o_ref.dtype)


def sb_decode(q, new_kv, cache, page_off, page_table, z):
    b, t, _ = q.shape
    n_pages = page_table.shape[1]
    rows = SB_HEADS * t
    tiles_per_key = 2 * SB_HEADS // SUBLANES
    tok = lambda w: pl.BlockSpec((1, t, w), lambda bi, pt: (bi, 0, 0))
    grid_spec = pltpu.PrefetchScalarGridSpec(
        num_scalar_prefetch=1,
        grid=(b,),
        in_specs=[tok(SB_W), tok(2 * SB_W), tok(SB_W), pl.BlockSpec(memory_space=pl.ANY)],
        out_specs=tok(SB_W),
        scratch_shapes=[
            pltpu.VMEM((2, tiles_per_key, PAGE_SIZE, SUBLANES, HEAD_DIM), F32),
            pltpu.SemaphoreType.DMA((2,)),
            pltpu.VMEM((rows, 1), F32),
            pltpu.VMEM((rows, HEAD_DIM), F32),
        ],
    )
    return pl.pallas_call(
        functools.partial(_sb_decode_kernel, t=t, n_pages=n_pages, page_off=page_off),
        grid_spec=grid_spec,
        out_shape=jax.ShapeDtypeStruct((b, t, SB_W), BF16),
        compiler_params=_cparams(("arbitrary",)),
        name="sb_decode",
    )(page_table, q, new_kv, z, cache)


def _online_update(s, mask, m, l):
    sm = jnp.where(mask, s, NEG)
    m_new = jnp.maximum(m, jnp.max(sm, axis=1, keepdims=True))
    alpha = jnp.exp(m - m_new)
    p = jnp.where(mask, jnp.exp(sm - m_new), 0.0)
    l_new = alpha * l + jnp.sum(p, axis=1, keepdims=True)
    return p, alpha, m_new, l_new


def _flash_step(s, v, m_ref, l_ref, acc_ref):
    m_old = m_ref[...]
    m_new = jnp.maximum(m_old, jnp.max(s, axis=1, keepdims=True))
    alpha = jnp.exp(m_old - m_new)
    p = jnp.exp(s - m_new)
    l_ref[...] = alpha * l_ref[...] + jnp.sum(p, axis=1, keepdims=True)
    m_ref[...] = m_new
    acc_ref[...] = alpha * acc_ref[...] + _dot(p.astype(BF16), v)


def _flash_step_t(s_t, v_t, m_ref, l_ref, acc_ref):
    m_old = m_ref[...]
    m_new = jnp.maximum(m_old, jnp.max(s_t, axis=0, keepdims=True))
    alpha = jnp.exp(m_old - m_new)
    p = jnp.exp(s_t - m_new)
    l_ref[...] = alpha * l_ref[...] + jnp.sum(p, axis=0, keepdims=True)
    m_ref[...] = m_new
    acc_ref[...] = alpha * acc_ref[...] + _dot(v_t, p.astype(BF16))


def _normalise(acc, l):
    return acc / jnp.maximum(l, 1e-30)


def _diff_lambda(lq1_ref, lk1_ref, lq2_ref, lk2_ref, lambda_init):
    d1 = jnp.sum(lq1_ref[...] * lk1_ref[...], axis=1, keepdims=True)
    d2 = jnp.sum(lq2_ref[...] * lk2_ref[...], axis=1, keepdims=True)
    return jnp.exp(d1) - jnp.exp(d2) + lambda_init


def _diff_finish(o, hg, zg, lambda_init):
    ms = jnp.mean(o * o, axis=-1, keepdims=True)
    o = o * lax.rsqrt(ms + EPS) * hg * (1.0 - lambda_init)
    return o * _silu(zg)


def _diff_prompt_kernel(slopes_ref, q_ref, k_ref, v_ref, z_ref, lq1_ref, lk1_ref, lq2_ref, lk2_ref, hg_ref, o_ref,
                        vt_ref, m_ref, l_ref, acc_ref, *, tq, tk, lambda_init):
    h = pl.program_id(1)
    i = pl.program_id(2)
    slope = slopes_ref[h]
    t_total = v_ref.shape[1]

    @pl.when(i == 0)
    def _():
        for c in range(t_total // tk):
            vt_ref[c] = v_ref[0, c * tk:(c + 1) * tk, :].T.astype(BF16)

    q = q_ref[0] * DIFF_HALF ** -0.5
    qs = [q[:, :DIFF_HALF].astype(BF16), q[:, DIFF_HALF:].astype(BF16)]
    rc = _iota((tk, tq), 1) - _iota((tk, tq), 0)
    bias_rc = slope * rc.astype(F32)
    m_ref[...] = jnp.full_like(m_ref, NEG)
    l_ref[...] = jnp.zeros_like(l_ref)
    acc_ref[...] = jnp.zeros_like(acc_ref)

    def chunk(j, masked):
        start = pl.multiple_of(j * tk, tk)
        k = k_ref[0, pl.ds(start, tk), :]
        vt = vt_ref[j]
        off = i * tq - j * tk
        bias = bias_rc + slope * off.astype(F32)
        for c in range(2):
            s = _dot_nt(k[:, c * DIFF_HALF:(c + 1) * DIFF_HALF].astype(BF16), qs[c]) - bias
            if masked:
                s = jnp.where(rc + off >= 0, s, NEG)
            _flash_step_t(s, vt, m_ref.at[c], l_ref.at[c], acc_ref.at[c])

    n_diag = tq // tk

    def body(j, carry):
        chunk(j, False)
        return carry

    lax.fori_loop(0, i * n_diag, body, 0)
    for d in range(n_diag):
        chunk(i * n_diag + d, True)
    lam = _diff_lambda(lq1_ref, lk1_ref, lq2_ref, lk2_ref, lambda_init)
    o_t = _normalise(acc_ref[0], l_ref[0]) - lam * _normalise(acc_ref[1], l_ref[1])
    o_ref[0] = _diff_finish(o_t.T, hg_ref[...], z_ref[0], lambda_init).astype(o_ref.dtype)


def _smem_spec():
    return pl.BlockSpec(memory_space=pltpu.SMEM)


def diff_prompt(q, kv, z, lq1, lk1, lq2, lk2, head_g, lambda_init, *, tq=256, tk=256):
    b, t, _ = q.shape
    tq = min(tq, t)
    tk = min(tk, tq)
    hq = pl.BlockSpec((1, tq, DIFF_VDIM), lambda bi, h, i: (bi, i, h))
    vec = lambda w: pl.BlockSpec((1, w), lambda bi, h, i: (0, 0))
    return pl.pallas_call(
        functools.partial(_diff_prompt_kernel, tq=tq, tk=tk, lambda_init=lambda_init),
        grid=(b, DIFF_HEADS, t // tq),
        in_specs=[
            _smem_spec(),
            hq,
            pl.BlockSpec((1, t, DIFF_VDIM), lambda bi, h, i: (bi, 0, h)),
            pl.BlockSpec((1, t, DIFF_VDIM), lambda bi, h, i: (bi, 0, DIFF_HEADS + h)),
            hq,
            vec(DIFF_HALF), vec(DIFF_HALF), vec(DIFF_HALF), vec(DIFF_HALF), vec(DIFF_VDIM),
        ],
        out_specs=hq,
        out_shape=jax.ShapeDtypeStruct((b, t, C_W), BF16),
        scratch_shapes=[pltpu.VMEM((t // tk, DIFF_VDIM, tk), BF16),
                        pltpu.VMEM((2, 1, tq), F32), pltpu.VMEM((2, 1, tq), F32),
                        pltpu.VMEM((2, DIFF_VDIM, tq), F32)],
        compiler_params=_cparams(("parallel", "parallel", "arbitrary")),
        name="diff_prompt",
    )(jnp.asarray(_alibi_slopes(DIFF_HEADS)), q, kv, kv, z,
      lq1.reshape(1, -1), lk1.reshape(1, -1), lq2.reshape(1, -1), lk2.reshape(1, -1), head_g.reshape(1, -1))


def _diff_decode_kernel(pt_ref, q_ref, knew_ref, vnew_ref, kpage_ref, vpage_ref, z_ref, slope_ref, lq1_ref, lk1_ref,
                        lq2_ref, lk2_ref, hg_ref, o_ref, qs_ref, bias_ref, biasn_ref, m_ref, l_ref, acc_ref,
                        *, t, n_pages, lambda_init):
    p = pl.program_id(1)
    scale = DIFF_HALF ** -0.5
    n_tiles = DIFF_HEADS // SUBLANES
    hr = SUBLANES * t
    past = n_pages * PAGE_SIZE

    def tile_bias(a, n_keys, causal):
        shape = (hr, n_keys * SUBLANES)
        r, c = _iota(shape, 0), _iota(shape, 1)
        ok = (r // t) == (c % SUBLANES)
        rel = r % t - c // SUBLANES
        if causal:
            ok = ok & (rel >= 0)
        return jnp.where(ok, -slope_ref[a] * rel.astype(F32), NEG)

    def attend(a, get_k, v, bias):
        s = jnp.concatenate([_dot_nt(qs_ref[c, a], get_k(c).astype(BF16)) + bias for c in range(2)], axis=0)
        m_old = m_ref[a]
        m_new = jnp.maximum(m_old, jnp.max(s, axis=1, keepdims=True))
        alpha = jnp.exp(m_old - m_new)
        pr = jnp.exp(s - m_new)
        l_ref[a] = alpha * l_ref[a] + jnp.sum(pr, axis=1, keepdims=True)
        m_ref[a] = m_new
        acc_ref[a] = alpha * acc_ref[a] + _dot(pr.astype(BF16), v.astype(BF16))

    @pl.when(p == 0)
    def _():
        q = q_ref[0] * scale
        for a in range(n_tiles):
            for c in range(2):
                cols = [(a * SUBLANES + hl) * DIFF_VDIM + c * DIFF_HALF for hl in range(SUBLANES)]
                qs_ref[c, a] = jnp.concatenate([q[:, o:o + DIFF_HALF] for o in cols], axis=0).astype(BF16)
            bias_ref[a] = tile_bias(a, PAGE_SIZE, False)
            biasn_ref[a] = tile_bias(a, t, True)
        m_ref[...] = jnp.full_like(m_ref, NEG)
        l_ref[...] = jnp.zeros_like(l_ref)
        acc_ref[...] = jnp.zeros_like(acc_ref)

    @pl.when(p < n_pages)
    def _():
        base = (past - p * PAGE_SIZE).astype(F32)
        for a in range(n_tiles):
            rows = slice(a * SUBLANES, (a + 1) * SUBLANES)
            get_k = lambda c: kpage_ref[:, rows, c * DIFF_HALF:(c + 1) * DIFF_HALF].reshape(
                PAGE_SIZE * SUBLANES, DIFF_HALF)
            v = vpage_ref[:, rows, :].reshape(PAGE_SIZE * SUBLANES, DIFF_VDIM)
            attend(a, get_k, v, bias_ref[a] - slope_ref[a] * base)

    @pl.when(p == n_pages)
    def _():
        lam = _diff_lambda(lq1_ref, lk1_ref, lq2_ref, lk2_ref, lambda_init)
        zg = z_ref[0]
        hg = hg_ref[...]
        for a in range(n_tiles):
            rows = slice(a * SUBLANES, (a + 1) * SUBLANES)
            get_k = lambda c: knew_ref[0, :, rows, c * DIFF_HALF:(c + 1) * DIFF_HALF].reshape(t * SUBLANES, DIFF_HALF)
            attend(a, get_k, vnew_ref[0, :, rows, :].reshape(t * SUBLANES, DIFF_VDIM), biasn_ref[a])
            o = _normalise(acc_ref[a], l_ref[a])
            for hl in range(SUBLANES):
                r1 = slice(hl * t, (hl + 1) * t)
                r2 = slice(hr + hl * t, hr + (hl + 1) * t)
                sl = slice((a * SUBLANES + hl) * DIFF_VDIM, (a * SUBLANES + hl + 1) * DIFF_VDIM)
                o_ref[0, :, sl] = _diff_finish(o[r1] - lam * o[r2], hg, zg[:, sl], lambda_init).astype(o_ref.dtype)


def diff_decode(q, new_kv, cache, page_off, page_table, z, lq1, lk1, lq2, lk2, head_g, lambda_init):
    b, t, _ = q.shape
    n_pages = page_table.shape[1]
    n_tiles = DIFF_HEADS // SUBLANES
    hr = SUBLANES * t
    slopes = np.repeat(_alibi_slopes(DIFF_HEADS), t).reshape(n_tiles, hr, 1)
    tok = lambda w: pl.BlockSpec((1, t, w), lambda bi, p, pt: (bi, 0, 0))
    vec = lambda w: pl.BlockSpec((1, w), lambda bi, p, pt: (0, 0))
    new_spec = lambda kv: pl.BlockSpec((1, t, DIFF_HEADS, DIFF_VDIM), lambda bi, p, pt: (bi, 0, kv, 0))
    page_spec = lambda kv: pl.BlockSpec(
        (None, PAGE_SIZE, DIFF_HEADS, DIFF_VDIM),
        lambda bi, p, pt: (page_off + pt[bi, jnp.minimum(p, n_pages - 1)], 0, kv, 0))
    grid_spec = pltpu.PrefetchScalarGridSpec(
        num_scalar_prefetch=1,
        grid=(b, n_pages + 1),
        in_specs=[
            tok(C_W), new_spec(0), new_spec(1), page_spec(0), page_spec(1), tok(C_W),
            pl.BlockSpec((n_tiles, hr, 1), lambda bi, p, pt: (0, 0, 0)),
            vec(DIFF_HALF), vec(DIFF_HALF), vec(DIFF_HALF), vec(DIFF_HALF), vec(DIFF_VDIM),
        ],
        out_specs=tok(C_W),
        scratch_shapes=[
            pltpu.VMEM((2, n_tiles, hr, DIFF_HALF), BF16),
            pltpu.VMEM((n_tiles, hr, PAGE_SIZE * SUBLANES), F32),
            pltpu.VMEM((n_tiles, hr, t * SUBLANES), F32),
            pltpu.VMEM((n_tiles, 2 * hr, 1), F32),
            pltpu.VMEM((n_tiles, 2 * hr, 1), F32),
            pltpu.VMEM((n_tiles, 2 * hr, DIFF_VDIM), F32),
        ],
    )
    return pl.pallas_call(
        functools.partial(_diff_decode_kernel, t=t, n_pages=n_pages, lambda_init=lambda_init),
        grid_spec=grid_spec,
        out_shape=jax.ShapeDtypeStruct((b, t, C_W), BF16),
        compiler_params=_cparams(("parallel", "arbitrary")),
        name="diff_decode",
    )(page_table, q, new_kv, new_kv, cache, cache, z, jnp.asarray(slopes),
      lq1.reshape(1, -1), lk1.reshape(1, -1), lq2.reshape(1, -1), lk2.reshape(1, -1), head_g.reshape(1, -1))


def _masked_softmax_parts(parts):
    m = None
    for s, mask in parts:
        pm = jnp.max(jnp.where(mask, s, NEG), axis=1, keepdims=True)
        m = pm if m is None else jnp.maximum(m, pm)
    es = [jnp.where(mask, jnp.exp(jnp.where(mask, s, NEG) - m), 0.0) for s, mask in parts]
    den = None
    for e in es:
        d = jnp.sum(e, axis=1, keepdims=True)
        den = d if den is None else den + d
    den = jnp.maximum(den, 1e-30)
    return [e / den for e in es]


def _nsa_prompt_kernel(slopes_ref, q_ref, kc_ref, vc_ref, ks_ref, vs_ref, kw_ref, vw_ref, lg_ref, z_ref,
                       wk_ref, wv_ref, expand_ref, o_ref, kcmp_ref, vcmp_ref, vst_ref, vwt_ref, selbias_ref, m_ref,
                       l_ref, acc_ref, *, tq, t_total):
    g = pl.program_id(1)
    i = pl.program_id(2)
    scale = HEAD_DIM ** -0.5
    nb = t_total // SEL_BLOCK
    rows = NSA_GROUP * tq
    tk = tq

    @pl.when(i == 0)
    def _():
        for src, w_ref, dst in ((kc_ref, wk_ref, kcmp_ref), (vc_ref, wv_ref, vcmp_ref)):
            x = src[0].reshape(nb, SEL_BLOCK, HEAD_DIM)
            w = w_ref[0][None]
            even = jnp.sum(x[:, :CMP_BLOCK, :] * w, axis=1)
            odd = jnp.sum(x[:, CMP_BLOCK:, :] * w, axis=1)
            pad = jnp.zeros((LANES - 2 * nb, HEAD_DIM), F32)
            dst[...] = jnp.concatenate([even, odd, pad], axis=0).astype(BF16)
        for src, dst in ((vs_ref, vst_ref), (vw_ref, vwt_ref)):
            for c in range(t_total // tk):
                dst[c] = src[0, c * tk:(c + 1) * tk, :].T.astype(BF16)

    q = q_ref[0] * scale
    q4 = jnp.concatenate([q[:, zz * HEAD_DIM:(zz + 1) * HEAD_DIM] for zz in range(NSA_GROUP)], axis=0).astype(BF16)
    qpos1 = i * tq + _iota((tq, 1), 0)
    qpos4 = jnp.concatenate([qpos1] * NSA_GROUP, axis=0)
    slope4 = jnp.concatenate(
        [jnp.full((tq, 1), slopes_ref[g * NSA_GROUP + zz], F32) for zz in range(NSA_GROUP)], axis=0)
    col = _iota((1, LANES), 1)

    cidx = jnp.where(col < nb, 2 * col, 2 * (col - nb) + 1)
    dist_c = qpos4 - (cidx * CMP_BLOCK + (CMP_BLOCK - 1))
    s_c = _dot_nt(q4, kcmp_ref[...]) - slope4 * dist_c.astype(F32)
    (p_c,) = _masked_softmax_parts([(s_c, (dist_c >= 0) & (col < 2 * nb))])
    o_cmp = _dot(p_c.astype(BF16), vcmp_ref[...])
    imp = p_c[0:tq]
    for zz in range(1, NSA_GROUP):
        imp = imp + p_c[zz * tq:(zz + 1) * tq]
    pair = imp + pltpu.roll(imp, LANES - nb, 1)

    cur = qpos1 // SEL_BLOCK
    valid = col * SEL_BLOCK <= qpos1
    forced = (col == cur) | (col == 0)
    score = jnp.where(forced, FORCED_SCORE, jnp.where(valid, pair, -1.0))
    score = jnp.where(col < nb, score, -2.0)
    score_t = score.T
    cand = score_t[:nb]
    blk = _iota((nb, 1), 0)
    rank = jnp.zeros((nb, tq), F32)
    for j in range(nb):
        r = score_t[j:j + 1, :]
        ge = jnp.where(r >= cand, 1.0, 0.0)
        gt = jnp.where(r > cand, 1.0, 0.0)
        rank = rank + jnp.where(blk > j, ge, gt)
    sel_t = (rank < float(min(SEL_TOPK, nb))).astype(F32)
    sel_rows = jnp.concatenate([sel_t, jnp.zeros((LANES - nb, tq), F32)], axis=0).astype(BF16)
    selbias_ref[...] = (_dot(expand_ref[...], sel_rows) - 1.0) * (-NEG)

    tile = lambda x: jnp.concatenate([x] * NSA_GROUP, axis=1)
    rc4 = tile(_iota((tk, tq), 1) - _iota((tk, tq), 0))
    slope_row = jnp.concatenate(
        [jnp.full((1, tq), slopes_ref[g * NSA_GROUP + zz], F32) for zz in range(NSA_GROUP)], axis=1)
    bias_rc = slope_row * rc4.astype(F32)

    def reset():
        m_ref[...] = jnp.full_like(m_ref, NEG)
        l_ref[...] = jnp.zeros_like(l_ref)
        acc_ref[...] = jnp.zeros_like(acc_ref)

    def step(k_ref, vt_ref, j, admit):
        start = pl.multiple_of(j * tk, tk)
        k = k_ref[0, pl.ds(start, tk), :].astype(BF16)
        off = (i - j) * tk
        s = _dot_nt(k, q4) - (bias_rc + slope_row * off.astype(F32))
        _flash_step_t(admit(s, start, rc4 + off), vt_ref[j], m_ref, l_ref, acc_ref)

    def loop(n, fn):
        def body(c, carry):
            fn(c)
            return carry
        lax.fori_loop(0, n, body, 0)

    chosen = lambda s, start: s + tile(selbias_ref[pl.ds(start, tk), :])
    reset()
    loop(i, lambda j: step(ks_ref, vst_ref, j, lambda s, start, dist: chosen(s, start)))
    step(ks_ref, vst_ref, i, lambda s, start, dist: jnp.where(dist >= 0, chosen(s, start), NEG))
    o_slc_t = _normalise(acc_ref[...], l_ref[...])

    n_inside = WINDOW // tk - 1
    reset()
    step(kw_ref, vwt_ref, i, lambda s, start, dist: jnp.where(dist >= 0, s, NEG))
    loop(jnp.minimum(i, n_inside), lambda c: step(kw_ref, vwt_ref, i - 1 - c, lambda s, start, dist: s))

    @pl.when(i > n_inside)
    def _():
        step(kw_ref, vwt_ref, i - 1 - n_inside, lambda s, start, dist: jnp.where(dist < WINDOW, s, NEG))

    o_win_t = _normalise(acc_ref[...], l_ref[...])

    gates = pltpu.roll(_sigmoid(lg_ref[0]), (LANES - 3 * NSA_GROUP * g) % LANES, 1)
    zg = z_ref[0]
    for zz in range(NSA_GROUP):
        r = slice(zz * tq, (zz + 1) * tq)
        mix = (gates[:, 3 * zz:3 * zz + 1] * o_cmp[r] + gates[:, 3 * zz + 1:3 * zz + 2] * o_slc_t[:, r].T
               + gates[:, 3 * zz + 2:3 * zz + 3] * o_win_t[:, r].T)
        sl = slice(zz * HEAD_DIM, (zz + 1) * HEAD_DIM)
        o_ref[0, :, sl] = (mix * _silu(zg[:, sl])).astype(o_ref.dtype)


def _cmp_weight_rows(w):
    return jnp.broadcast_to(w.T[:, :, None], (NSA_KV_HEADS, CMP_BLOCK, HEAD_DIM)).astype(F32)


def nsa_prompt(q, nskv, win, logits, z, cmp_wk, cmp_wv, *, tq=128):
    b, t, _ = q.shape
    nb = t // SEL_BLOCK
    assert t % tq == 0 and 2 * nb <= LANES and WINDOW % tq == 0
    gw = NSA_GROUP * HEAD_DIM
    expand = ((np.arange(t)[:, None] // SEL_BLOCK) == np.arange(LANES)[None, :]).astype(np.float32)
    kvspec = lambda kind: pl.BlockSpec((1, t, HEAD_DIM), lambda bi, g, i: (bi, 0, kind * NSA_KV_HEADS + g))
    qspec = pl.BlockSpec((1, tq, gw), lambda bi, g, i: (bi, i, g))
    wspec = pl.BlockSpec((1, CMP_BLOCK, HEAD_DIM), lambda bi, g, i: (g, 0, 0))
    return pl.pallas_call(
        functools.partial(_nsa_prompt_kernel, tq=tq, t_total=t),
        grid=(b, NSA_KV_HEADS, t // tq),
        in_specs=[
            _smem_spec(),
            qspec,
            kvspec(0), kvspec(1), kvspec(2), kvspec(3),
            kvspec(0), kvspec(1),
            pl.BlockSpec((1, tq, LANES), lambda bi, g, i: (bi, i, 0)),
            qspec,
            wspec, wspec,
            pl.BlockSpec((t, LANES), lambda bi, g, i: (0, 0)),
        ],
        out_specs=qspec,
        out_shape=jax.ShapeDtypeStruct((b, t, NSA_W), BF16),
        scratch_shapes=[
            pltpu.VMEM((LANES, HEAD_DIM), BF16),
            pltpu.VMEM((LANES, HEAD_DIM), BF16),
            pltpu.VMEM((t // tq, HEAD_DIM, tq), BF16),
            pltpu.VMEM((t // tq, HEAD_DIM, tq), BF16),
            pltpu.VMEM((t, tq), F32),
            pltpu.VMEM((1, NSA_GROUP * tq), F32),
            pltpu.VMEM((1, NSA_GROUP * tq), F32),
            pltpu.VMEM((HEAD_DIM, NSA_GROUP * tq), F32),
        ],
        compiler_params=_cparams(("parallel", "parallel", "arbitrary")),
        name="nsa_prompt",
    )(jnp.asarray(_alibi_slopes(NSA_HEADS)), q, nskv, nskv, nskv, nskv, win, win, logits, z,
      _cmp_weight_rows(cmp_wk), _cmp_weight_rows(cmp_wv), jnp.asarray(expand, BF16))


NSA_PAGES_PER_STEP = 4


def _nsa_compress_kernel(pt_ref, *refs):
    page_refs, w_ref, even_ref, odd_ref = refs[:-3], refs[-3], refs[-2], refs[-1]
    n_cmp = PAGE_SIZE // CMP_BLOCK
    w = w_ref[...][None]
    evens, odds = [], []
    for page_ref in page_refs:
        x = page_ref[...].reshape(n_cmp, CMP_BLOCK, 2 * NSA_KV_HEADS, HEAD_DIM)
        c = jnp.sum(x * w, axis=1)
        evens += [c[r:r + 1] for r in range(0, n_cmp, 2)]
        odds += [c[r:r + 1] for r in range(1, n_cmp, 2)]
    even_ref[0] = jnp.concatenate(evens, axis=0)
    odd_ref[0] = jnp.concatenate(odds, axis=0)


def nsa_compress_pages(cache, page_off, page_table, cmp_wk, cmp_wv):
    b, n_pages = page_table.shape
    pps = math.gcd(n_pages, NSA_PAGES_PER_STEP)
    half_blocks = PAGE_SIZE // CMP_BLOCK // 2
    w = jnp.concatenate([cmp_wk, cmp_wv], axis=1).astype(F32)
    w = jnp.broadcast_to(w[:, :, None], (CMP_BLOCK, 2 * NSA_KV_HEADS, HEAD_DIM))
    cmp_spec = pl.BlockSpec((1, pps * half_blocks, 2 * NSA_KV_HEADS, HEAD_DIM), lambda bi, p, pt: (bi, p, 0, 0))
    page_spec = lambda k: pl.BlockSpec((None, PAGE_SIZE, 2 * NSA_KV_HEADS, HEAD_DIM),
                                       lambda bi, p, pt: (page_off + pt[bi, p * pps + k], 0, 0, 0))
    grid_spec = pltpu.PrefetchScalarGridSpec(
        num_scalar_prefetch=1,
        grid=(b, n_pages // pps),
        in_specs=[page_spec(k) for k in range(pps)]
        + [pl.BlockSpec((CMP_BLOCK, 2 * NSA_KV_HEADS, HEAD_DIM), lambda bi, p, pt: (0, 0, 0))],
        out_specs=[cmp_spec, cmp_spec],
    )
    return pl.pallas_call(
        _nsa_compress_kernel,
        grid_spec=grid_spec,
        out_shape=[jax.ShapeDtypeStruct((b, n_pages * half_blocks, 2 * NSA_KV_HEADS, HEAD_DIM), F32)] * 2,
        compiler_params=_cparams(("parallel", "parallel")),
        name="nsa_compress_pages",
    )(page_table, *([cache] * pps), w)


def _nsa_decode_kernel(pt_ref, q_ref, even_ref, odd_ref, *refs, t, n_pages, pps):
    page_refs = refs[:pps]
    (new_ref, wst_ref, wnew_ref, lg_ref, z_ref, slope_ref, o_ref, qg_ref, sel_ref, ocmp_ref, m_ref, l_ref,
     acc_ref) = refs[pps:]
    p = pl.program_id(1)
    n_steps = n_pages // pps
    scale = HEAD_DIM ** -0.5
    rows = NSA_HEADS * t
    grp_rows = NSA_GROUP * t
    past = n_pages * PAGE_SIZE
    n_blk = past // SEL_BLOCK
    half = n_blk
    tq = _iota((rows, 1), 0) % t
    qpos = past + tq
    slope = slope_ref[...]
    col = _iota((1, PAGE_SIZE), 1)

    def scores(get_k):
        return jnp.concatenate([_dot_nt(qg_ref[gg], get_k(gg).astype(BF16)) for gg in range(NSA_KV_HEADS)],
                               axis=0) * scale

    def group_pv(pr, get_v):
        return jnp.concatenate([_dot(pr[gg * grp_rows:(gg + 1) * grp_rows], get_v(gg).astype(BF16))
                                for gg in range(NSA_KV_HEADS)], axis=0)

    def rows_kv(ref3):
        return (lambda gg: ref3[:, gg, :]), (lambda gg: ref3[:, NSA_KV_HEADS + gg, :])

    def lanes_kv(x, first_block):
        blk = lambda i: x[:, (first_block + i) * HEAD_DIM:(first_block + i + 1) * HEAD_DIM]
        return (lambda gg: blk(gg)), (lambda gg: blk(NSA_KV_HEADS + gg))

    @pl.when(p == 0)
    def _():
        q = q_ref[0]
        for gg in range(NSA_KV_HEADS):
            heads = range(gg * NSA_GROUP, (gg + 1) * NSA_GROUP)
            qg_ref[gg] = jnp.concatenate([q[:, h * HEAD_DIM:(h + 1) * HEAD_DIM] for h in heads], axis=0).astype(BF16)
        parts, vals = [], []
        for par, cmp_ref in enumerate((even_ref, odd_ref)):
            get_k, get_v = rows_kv(cmp_ref.at[0])
            vals.append(get_v)
            c_end = (2 * _iota((1, half), 1) + par) * CMP_BLOCK + (CMP_BLOCK - 1)
            dist = qpos - c_end
            parts.append((scores(get_k) - slope * dist.astype(F32), dist >= 0))
        p_e, p_o = _masked_softmax_parts(parts)
        ocmp_ref[...] = group_pv(p_e.astype(BF16), vals[0]) + group_pv(p_o.astype(BF16), vals[1])
        pe = p_e + p_o
        blkcol = _iota((1, n_blk), 1)
        picked = []
        for gg in range(NSA_KV_HEADS):
            imp = pe[gg * grp_rows:gg * grp_rows + t]
            for zz in range(1, NSA_GROUP):
                imp = imp + pe[gg * grp_rows + zz * t:gg * grp_rows + (zz + 1) * t]
            sc = jnp.where(blkcol == 0, -1.0, imp)
            sel = (blkcol == 0)
            for _ in range(min(SEL_TOPK, n_blk + 1) - 2):
                mx = jnp.max(sc, axis=1, keepdims=True)
                first = jnp.min(jnp.where(sc == mx, blkcol, n_blk), axis=1, keepdims=True)
                hit = blkcol == first
                sel = sel | hit
                sc = jnp.where(hit, -1.0, sc)
            self32 = sel.astype(F32)
            picked.extend([self32] * NSA_GROUP)
        sel_ref[...] = jnp.concatenate(picked, axis=0).astype(BF16)
        m_ref[...] = jnp.full_like(m_ref, NEG)
        l_ref[...] = jnp.zeros_like(l_ref)
        acc_ref[...] = jnp.zeros_like(acc_ref)

    def attend(get_k, get_v, dist, mask):
        s = scores(get_k) - slope * dist.astype(F32)
        pr, alpha, m_new, l_new = _online_update(s, mask, m_ref[...], l_ref[...])
        m_ref[...] = m_new
        l_ref[...] = l_new
        acc_ref[...] = alpha * acc_ref[...] + group_pv(pr.astype(BF16), get_v)

    @pl.when(p < n_steps)
    def _():
        keys = pps * PAGE_SIZE
        blocks_per_step = keys // SEL_BLOCK
        expand = (_iota((n_blk, keys), 0) == blocks_per_step * p + _iota((n_blk, keys), 1) // SEL_BLOCK)
        chosen = _dot(sel_ref[...], expand.astype(BF16)) > 0.5
        dist = qpos - (p * keys + _iota((1, keys), 1))
        getters = [rows_kv(page_ref) for page_ref in page_refs]
        get_k = lambda gg: jnp.concatenate([gk(gg) for gk, _ in getters], axis=0)
        get_v = lambda gg: jnp.concatenate([gv(gg) for _, gv in getters], axis=0)
        attend(get_k, get_v, dist, chosen & (dist >= 0))

    @pl.when(p == n_steps)
    def _():
        new = _pad_rows(new_ref[0], PAGE_SIZE)
        dist = qpos - (past + col)
        get_k, get_v = lanes_kv(new, 2 * NSA_KV_HEADS)
        attend(get_k, get_v, dist, (dist >= 0) & (col < t))
        o_slc = _normalise(acc_ref[...], l_ref[...])
        n_state = wst_ref.shape[1]
        dist_s = qpos - (past - n_state + _iota((1, n_state), 1))
        ks_state, vs_state = rows_kv(wst_ref.at[0])
        ks_new, vs_new = lanes_kv(_pad_rows(wnew_ref[0], PAGE_SIZE), 0)
        p_s, p_n = _masked_softmax_parts([
            (scores(ks_state) - slope * dist_s.astype(F32), (dist_s >= 0) & (dist_s < WINDOW)),
            (scores(ks_new) - slope * dist.astype(F32), (dist >= 0) & (dist < WINDOW) & (col < t)),
        ])
        o_win = group_pv(p_s.astype(BF16), vs_state) + group_pv(p_n.astype(BF16), vs_new)
        o_cmp = ocmp_ref[...]
        gates = _sigmoid(lg_ref[0])
        zg = z_ref[0]
        for h in range(NSA_HEADS):
            r = slice(h * t, (h + 1) * t)
            mix = (gates[:, 3 * h:3 * h + 1] * o_cmp[r] + gates[:, 3 * h + 1:3 * h + 2] * o_slc[r]
                   + gates[:, 3 * h + 2:3 * h + 3] * o_win[r])
            sl = slice(h * HEAD_DIM, (h + 1) * HEAD_DIM)
            o_ref[0, :, sl] = (mix * _silu(zg[:, sl])).astype(o_ref.dtype)


def nsa_decode(q, cmp_even, cmp_odd, cache, page_off, page_table, new_nskv, win_state, win_new, logits, z):
    b, t, _ = q.shape
    n_pages = page_table.shape[1]
    past = n_pages * PAGE_SIZE
    n_blk = past // SEL_BLOCK
    n_state = win_state.shape[1]
    assert past % SEL_BLOCK == 0 and t < CMP_BLOCK and n_blk + 1 >= SEL_TOPK and n_state >= WINDOW - 1
    rows = NSA_HEADS * t
    slopes = np.repeat(_alibi_slopes(NSA_HEADS), t).reshape(rows, 1)
    pps = math.gcd(n_pages, NSA_PAGES_PER_STEP)
    n_steps = n_pages // pps
    tok = lambda w: pl.BlockSpec((1, t, w), lambda bi, p, pt: (bi, 0, 0))
    page_spec = lambda k: pl.BlockSpec(
        (None, PAGE_SIZE, 2 * NSA_KV_HEADS, HEAD_DIM),
        lambda bi, p, pt: (page_off + pt[bi, jnp.minimum(p, n_steps - 1) * pps + k], 0, 1, 0))
    grid_spec = pltpu.PrefetchScalarGridSpec(
        num_scalar_prefetch=1,
        grid=(b, n_steps + 1),
        in_specs=[
            tok(NSA_W),
            pl.BlockSpec((1, n_blk, 2 * NSA_KV_HEADS, HEAD_DIM), lambda bi, p, pt: (bi, 0, 0, 0)),
            pl.BlockSpec((1, n_blk, 2 * NSA_KV_HEADS, HEAD_DIM), lambda bi, p, pt: (bi, 0, 0, 0)),
            *[page_spec(k) for k in range(pps)],
            tok(4 * NSA_KV_W),
            pl.BlockSpec((1, n_state, 2 * NSA_KV_HEADS, HEAD_DIM), lambda bi, p, pt: (bi, 0, 0, 0)),
            tok(2 * NSA_KV_W),
            tok(LANES),
            tok(NSA_W),
            pl.BlockSpec((rows, 1), lambda bi, p, pt: (0, 0)),
        ],
        out_specs=tok(NSA_W),
        scratch_shapes=[
            pltpu.VMEM((NSA_KV_HEADS, NSA_GROUP * t, HEAD_DIM), BF16),
            pltpu.VMEM((rows, n_blk), BF16),
            pltpu.VMEM((rows, HEAD_DIM), F32),
            pltpu.VMEM((rows, 1), F32),
            pltpu.VMEM((rows, 1), F32),
            pltpu.VMEM((rows, HEAD_DIM), F32),
        ],
    )
    return pl.pallas_call(
        functools.partial(_nsa_decode_kernel, t=t, n_pages=n_pages, pps=pps),
        grid_spec=grid_spec,
        out_shape=jax.ShapeDtypeStruct((b, t, NSA_W), BF16),
        compiler_params=_cparams(("parallel", "arbitrary")),
        name="nsa_decode",
    )(page_table, q, cmp_even, cmp_odd, *([cache] * pps), new_nskv, win_state, win_new, logits, z, jnp.asarray(slopes))


def _ab_projections(hn, w_in):
    o = np.cumsum([0, SB_W, 2 * SB_W, SB_W, NSA_W, 4 * NSA_KV_W, 2 * NSA_KV_W, 3 * NSA_HEADS, NSA_W])
    names = ("sbq", "sbkv", "sbz", "nsq", "nskv", "win")
    proj = {name: matmul([hn], w_in, col0=int(o[c]), n=int(o[c + 1] - o[c])) for c, name in enumerate(names)}
    proj["logits"] = matmul([hn], jnp.pad(w_in[:, o[6]:o[7]], ((0, 0), (0, LANES - 3 * NSA_HEADS))))
    proj["nsz"] = matmul([hn], w_in[:, o[7]:o[8]])
    return proj


def _c_projections(hn, w_in):
    return dict(q=matmul([hn], w_in, col0=0, n=C_W), kv=matmul([hn], w_in, col0=C_W, n=2 * C_W),
                z=matmul([hn], w_in, col0=3 * C_W, n=C_W))


def kernel(x_prompt, x_sample, p_prompt, p_sample, cache_sb_kv, cache_nsa_kv, state_nsa_win_kv, cache_diff_kv,
           page_table, norm_g, w_in_ab, nsa_cmp_wk, nsa_cmp_wv, w_out_ab, w_in_c, diff_lq1, diff_lk1, diff_lq2,
           diff_lk2, diff_head_g, w_out_c, ple_norm_g, w_ple_gate, w_ple_proj, final_norm_g):
    bp, tp, d = x_prompt.shape
    bs, ts, _ = x_sample.shape
    depth = norm_g.shape[0]
    n_phys = cache_sb_kv.shape[1]

    def run(x, p_emb, sample):
        b, t, _ = x.shape
        h = x.reshape(b * t, d)
        sb_rows, nsa_rows, win_rows, diff_rows = [], [], [], []
        for i in range(depth):
            j = i // 2
            hn = rmsnorm(h, norm_g[i], BF16)
            if i % 2 == 0:
                proj = {name: y.reshape(b, t, -1) for name, y in _ab_projections(hn, w_in_ab[j]).items()}
                win_new = proj["win"].reshape(b, t, 2, NSA_KV_HEADS, HEAD_DIM)
                if sample:
                    sb_mixed = sb_decode(proj["sbq"], proj["sbkv"],
                                         cache_sb_kv.reshape(-1, PAGE_SIZE, 2 * SB_HEADS, HEAD_DIM), j * n_phys,
                                         page_table, proj["sbz"])
                    nsa_cache = cache_nsa_kv.reshape(-1, PAGE_SIZE, 4 * NSA_KV_HEADS, HEAD_DIM)
                    cmp_even, cmp_odd = nsa_compress_pages(nsa_cache, j * n_phys, page_table, nsa_cmp_wk[j],
                                                           nsa_cmp_wv[j])
                    win_rows_in = state_nsa_win_kv[j].reshape(b, -1, 2 * NSA_KV_HEADS, HEAD_DIM)
                    ns_mixed = nsa_decode(proj["nsq"], cmp_even, cmp_odd, nsa_cache, j * n_phys, page_table,
                                          proj["nskv"], win_rows_in, proj["win"], proj["logits"], proj["nsz"])
                    win_all = jnp.concatenate([state_nsa_win_kv[j], win_new], axis=1)
                else:
                    sb_mixed = sb_prompt(proj["sbq"], proj["sbkv"], proj["sbz"])
                    ns_mixed = nsa_prompt(proj["nsq"], proj["nskv"], proj["win"], proj["logits"], proj["nsz"],
                                          nsa_cmp_wk[j], nsa_cmp_wv[j])
                    win_all = win_new
                h = matmul([sb_mixed.reshape(b * t, -1), ns_mixed.reshape(b * t, -1)], w_out_ab[j], residual=h)
                sb_rows.append(proj["sbkv"].reshape(b, t, 2, SB_HEADS, HEAD_DIM))
                nsa_rows.append(proj["nskv"].reshape(b, t, 4, NSA_KV_HEADS, HEAD_DIM))
                keep = min(WINDOW, win_all.shape[1])
                win_rows.append(win_all[:, win_all.shape[1] - keep:])
            else:
                lambda_init = 0.8 - 0.6 * math.exp(-0.3 * i)
                proj = {name: y.reshape(b, t, -1) for name, y in _c_projections(hn, w_in_c[j]).items()}
                lam_args = (diff_lq1[j], diff_lk1[j], diff_lq2[j], diff_lk2[j], diff_head_g[j], lambda_init)
                if sample:
                    mixed = diff_decode(proj["q"], proj["kv"].reshape(b, t, 2 * DIFF_HEADS, DIFF_VDIM),
                                        cache_diff_kv.reshape(-1, PAGE_SIZE, 2 * DIFF_HEADS, DIFF_VDIM), j * n_phys,
                                        page_table, proj["z"], *lam_args)
                else:
                    mixed = diff_prompt(proj["q"], proj["kv"], proj["z"], *lam_args)
                h = matmul([mixed.reshape(b * t, -1)], w_out_c[j], residual=h)
                diff_rows.append(proj["kv"].reshape(b, t, 2, DIFF_HEADS, DIFF_VDIM))
            hn2 = rmsnorm(h, ple_norm_g[i], BF16)
            h = ple(hn2, w_ple_gate[i], p_emb[i].reshape(b * t, -1).astype(BF16), w_ple_proj[i], h)
        y = rmsnorm(h, final_norm_g, F32).reshape(b, t, d)
        return y, jnp.stack(sb_rows), jnp.stack(nsa_rows), jnp.stack(win_rows), jnp.stack(diff_rows)

    y_p, sb_p, nsa_p, win_p, diff_p = run(x_prompt, p_prompt, False)
    y_s, sb_s, nsa_s, win_s, diff_s = run(x_sample, p_sample, True)
    return (y_p, y_s, sb_p, sb_s, nsa_p, nsa_s, win_p, win_s, diff_p, diff_s)
```

```python
import functools
import math

import jax
import jax.numpy as jnp
import numpy as np
from jax import lax
from jax.experimental import pallas as pl
from jax.experimental.pallas import tpu as pltpu

F32 = jnp.float32
BF16 = jnp.bfloat16

HEAD_DIM = 128
SB_HEADS = 16
NSA_HEADS = 16
NSA_KV_HEADS = 4
NSA_GROUP = NSA_HEADS // NSA_KV_HEADS
CMP_BLOCK = 32
SEL_BLOCK = 64
SEL_TOPK = 16
WINDOW = 512
DIFF_HEADS = 16
DIFF_HALF = 128
DIFF_VDIM = 2 * DIFF_HALF
PAGE_SIZE = 128
EPS = 1e-6
NEG = -1e30
FORCED_SCORE = 1e4

SB_W = SB_HEADS * HEAD_DIM
NSA_W = NSA_HEADS * HEAD_DIM
NSA_KV_W = NSA_KV_HEADS * HEAD_DIM
C_W = DIFF_HEADS * DIFF_VDIM
LANES = 128
SUBLANES = 8
EXP_UNDERFLOW = -110.0
VMEM_LIMIT = 56 * 1024 * 1024


def _alibi_slopes(n):
    return np.asarray(2.0 ** (-8.0 * np.arange(1, n + 1) / n), dtype=np.float32)


def _cparams(sem):
    return pltpu.CompilerParams(dimension_semantics=sem, vmem_limit_bytes=VMEM_LIMIT)


def _dot(a, b):
    return jnp.dot(a, b, preferred_element_type=F32)


def _dot_nt(a, b):
    return lax.dot_general(a, b, (((1,), (1,)), ((), ())), preferred_element_type=F32)


def _silu(z):
    return z * (1.0 / (1.0 + jnp.exp(-z)))


def _sigmoid(z):
    return 1.0 / (1.0 + jnp.exp(-z))


def _iota(shape, dim):
    return lax.broadcasted_iota(jnp.int32, shape, dim)


def _rmsnorm_kernel(x_ref, g_ref, o_ref):
    x = x_ref[...]
    ms = jnp.mean(x * x, axis=-1, keepdims=True)
    o_ref[...] = (x * lax.rsqrt(ms + EPS) * g_ref[...]).astype(o_ref.dtype)


def rmsnorm(x, g, out_dtype):
    m, d = x.shape
    tm = min(m, 256)
    return pl.pallas_call(
        _rmsnorm_kernel,
        grid=(m // tm,),
        in_specs=[pl.BlockSpec((tm, d), lambda i: (i, 0)), pl.BlockSpec((1, d), lambda i: (0, 0))],
        out_specs=pl.BlockSpec((tm, d), lambda i: (i, 0)),
        out_shape=jax.ShapeDtypeStruct((m, d), out_dtype),
        compiler_params=_cparams(("parallel",)),
        name="rmsnorm",
    )(x, g.reshape(1, d))


def _mm_kernel(*refs, n_parts, has_residual):
    a_refs, w_refs = refs[:n_parts], refs[n_parts:2 * n_parts]
    r_ref = refs[2 * n_parts] if has_residual else None
    o_ref = refs[2 * n_parts + has_residual]
    wb_refs = refs[2 * n_parts + has_residual + 1:]

    @pl.when(pl.program_id(1) == 0)
    def _():
        for w_ref, wb_ref in zip(w_refs, wb_refs):
            wb_ref[...] = w_ref[...].astype(BF16)

    acc = _dot(a_refs[0][...], wb_refs[0][...])
    for a_ref, wb_ref in zip(a_refs[1:], wb_refs[1:]):
        acc = acc + _dot(a_ref[...], wb_ref[...])
    if has_residual:
        acc = r_ref[...] + acc
    o_ref[...] = acc


def matmul(a_parts, w, *, col0=0, n=None, residual=None):
    m, kp = a_parts[0].shape
    n_parts = len(a_parts)
    assert all(a.shape == (m, kp) for a in a_parts) and w.shape[0] == n_parts * kp
    n = w.shape[1] - col0 if n is None else n
    tm = min(m, 1024)
    tn = min(n, 512)
    assert m % tm == 0 and n % tn == 0 and col0 % tn == 0
    in_specs = [pl.BlockSpec((tm, kp), lambda j, i: (i, 0)) for _ in a_parts]
    in_specs += [pl.BlockSpec((kp, tn), lambda j, i, part=part: (part, col0 // tn + j)) for part in range(n_parts)]
    args = list(a_parts) + [w] * n_parts
    if residual is not None:
        in_specs.append(pl.BlockSpec((tm, tn), lambda j, i: (i, j)))
        args.append(residual)
    return pl.pallas_call(
        functools.partial(_mm_kernel, n_parts=n_parts, has_residual=residual is not None),
        grid=(n // tn, m // tm),
        in_specs=in_specs,
        out_specs=pl.BlockSpec((tm, tn), lambda j, i: (i, j)),
        out_shape=jax.ShapeDtypeStruct((m, n), F32),
        scratch_shapes=[pltpu.VMEM((kp, tn), BF16) for _ in a_parts],
        compiler_params=_cparams(("parallel", "arbitrary")),
        name="matmul",
    )(*args)


def _ple_kernel(hn_ref, wg_ref, p_ref, wp_ref, h_ref, o_ref, wgb_ref):
    @pl.when(pl.program_id(1) == 0)
    def _():
        wgb_ref[...] = wg_ref[...].astype(BF16)

    gate = _sigmoid(_dot(hn_ref[...], wgb_ref[...]))
    proj = _dot(p_ref[...], wp_ref[...].astype(BF16))
    o_ref[...] = h_ref[...] + gate * proj


def ple(hn, wg, p, wp, h):
    m, d = hn.shape
    n = wg.shape[1]
    pd = p.shape[1]
    tm = min(m, 1024)
    tn = min(n, 512)
    return pl.pallas_call(
        _ple_kernel,
        grid=(n // tn, m // tm),
        in_specs=[
            pl.BlockSpec((tm, d), lambda j, i: (i, 0)),
            pl.BlockSpec((d, tn), lambda j, i: (0, j)),
            pl.BlockSpec((tm, pd), lambda j, i: (i, 0)),
            pl.BlockSpec((pd, tn), lambda j, i: (0, j)),
            pl.BlockSpec((tm, tn), lambda j, i: (i, j)),
        ],
        out_specs=pl.BlockSpec((tm, tn), lambda j, i: (i, j)),
        out_shape=jax.ShapeDtypeStruct((m, n), F32),
        scratch_shapes=[pltpu.VMEM((d, tn), BF16)],
        compiler_params=_cparams(("parallel", "arbitrary")),
        name="ple",
    )(hn, wg, p, wp, h)


def _suffix_sum_matrix(c):
    return (_iota((c, c), 0) >= _iota((c, c), 1)).astype(BF16)


def _sb_prompt_kernel(q_ref, k_ref, v_ref, z_ref, o_ref, carry_ref, acc_ref, *, tq, tk):
    i = pl.program_id(2)
    qb = (q_ref[0] * HEAD_DIM ** -0.5).astype(BF16)
    incl_mat = _suffix_sum_matrix(tk)
    rc = _iota((tq, tk), 0) - _iota((tq, tk), 1)
    carry_ref[...] = jnp.zeros_like(carry_ref)
    acc_ref[...] = jnp.zeros_like(acc_ref)

    def chunk(j, masked):
        start = pl.multiple_of(j * tk, tk)
        k = k_ref[0, pl.ds(start, tk), :].astype(BF16)
        v = v_ref[0, pl.ds(start, tk), :].astype(BF16)
        z = _dot_nt(qb, k)
        lb = jnp.minimum(z, 0.0) - jnp.log(1.0 + jnp.exp(-jnp.abs(z)))
        lk = lb - z
        if masked:
            mask = rc > j * tk - i * tq
            lb = jnp.where(mask, lb, 0.0)
            lk = jnp.where(mask, lk, 0.0)
        hi = lk.astype(BF16)
        lo = (lk - hi.astype(F32)).astype(BF16)
        incl = _dot(hi, incl_mat) + _dot(lo, incl_mat)
        carry = carry_ref[...]
        a = jnp.exp(lb + (incl - lk + carry))
        if masked:
            a = jnp.where(mask, a, 0.0)
        acc_ref[...] += _dot(a.astype(BF16), v)
        carry_ref[...] = carry + incl[:, 0:1]

    n_diag = tq // tk
    top = (i + 1) * n_diag - 1
    for d in range(n_diag):
        chunk(top - d, True)

    def body(state):
        j, _ = state
        chunk(j, False)
        live = jnp.max(carry_ref[...]) > EXP_UNDERFLOW
        return j - 1, live.astype(jnp.int32)

    lax.while_loop(lambda st: (st[0] >= 0) & (st[1] > 0), body, (top - n_diag, jnp.int32(1)))
    o_ref[0] = (acc_ref[...] * _silu(z_ref[0])).astype(o_ref.dtype)


def sb_prompt(q, kv, z, *, tq=512, tk=256):
    b, t, _ = q.shape
    tq = min(tq, t)
    tk = min(tk, tq)
    hq = pl.BlockSpec((1, tq, HEAD_DIM), lambda bi, h, i: (bi, i, h))
    return pl.pallas_call(
        functools.partial(_sb_prompt_kernel, tq=tq, tk=tk),
        grid=(b, SB_HEADS, t // tq),
        in_specs=[
            hq,
            pl.BlockSpec((1, t, HEAD_DIM), lambda bi, h, i: (bi, 0, h)),
            pl.BlockSpec((1, t, HEAD_DIM), lambda bi, h, i: (bi, 0, SB_HEADS + h)),
            hq,
        ],
        out_specs=hq,
        out_shape=jax.ShapeDtypeStruct((b, t, SB_W), BF16),
        scratch_shapes=[pltpu.VMEM((tq, 1), F32), pltpu.VMEM((tq, HEAD_DIM), F32)],
        compiler_params=_cparams(("parallel", "parallel", "arbitrary")),
        name="sb_prompt",
    )(q, kv, kv, z)


def _pad_rows(x, rows):
    return jnp.concatenate([x, jnp.zeros((rows - x.shape[0], x.shape[1]), x.dtype)], axis=0)


def _sb_decode_kernel(pt_ref, q_ref, new_ref, z_ref, cache_ref, o_ref, buf_ref, sem_ref, carry_ref, acc_ref,
                      *, t, n_pages, page_off):
    b = pl.program_id(0)
    scale = HEAD_DIM ** -0.5
    rows = SB_HEADS * t
    tiles_per_key = 2 * SB_HEADS // SUBLANES
    tq = _iota((rows, 1), 0) % t
    col = _iota((1, PAGE_SIZE), 1)
    qb = q_ref[0].astype(BF16)

    def page_copies(slot, n):
        page = page_off + pt_ref[b, n_pages - 1 - n]
        return [pltpu.make_async_copy(cache_ref.at[page, :, pl.ds(r * SUBLANES, SUBLANES), :],
                                      buf_ref.at[slot, r], sem_ref.at[slot]) for r in range(tiles_per_key)]

    def attend(get_k, get_v, mask):
        z = jnp.concatenate([_dot_nt(qb[:, h * HEAD_DIM:(h + 1) * HEAD_DIM], get_k(h).astype(BF16))
                             for h in range(SB_HEADS)], axis=0) * scale
        carry = carry_ref[...]
        lb = jnp.minimum(z, 0.0) - jnp.log(1.0 + jnp.exp(-jnp.abs(z)))
        lk = lb - z
        if mask is not None:
            lb = jnp.where(mask, lb, 0.0)
            lk = jnp.where(mask, lk, 0.0)
        incl_mat = _suffix_sum_matrix(PAGE_SIZE)
        hi = lk.astype(BF16)
        lo = (lk - hi.astype(F32)).astype(BF16)
        incl = _dot(hi, incl_mat) + _dot(lo, incl_mat)
        a = jnp.exp(lb + (incl - lk + carry))
        if mask is not None:
            a = jnp.where(mask, a, 0.0)
        a = a.astype(BF16)
        for h in range(SB_HEADS):
            acc_ref[h * t:(h + 1) * t, :] += _dot(a[h * t:(h + 1) * t, :], get_v(h).astype(BF16))
        carry_ref[...] = carry + incl[:, 0:1]

    carry_ref[...] = jnp.zeros_like(carry_ref)
    acc_ref[...] = jnp.zeros_like(acc_ref)
    for cp in page_copies(0, 0):
        cp.start()
    new = _pad_rows(new_ref[0], PAGE_SIZE)
    attend(lambda h: new[:, h * HEAD_DIM:(h + 1) * HEAD_DIM],
           lambda h: new[:, SB_W + h * HEAD_DIM:SB_W + (h + 1) * HEAD_DIM], col < tq)

    def body(state):
        n, _ = state
        slot = n % 2
        for cp in page_copies(slot, n):
            cp.wait()

        @pl.when(n + 1 < n_pages)
        def _():
            for cp in page_copies(1 - slot, n + 1):
                cp.start()

        attend(lambda h: buf_ref[slot, h // SUBLANES, :, h % SUBLANES, :],
               lambda h: buf_ref[slot, (SB_HEADS + h) // SUBLANES, :, h % SUBLANES, :], None)
        live = jnp.max(carry_ref[...]) > EXP_UNDERFLOW
        return n + 1, live.astype(jnp.int32)

    n_done, _ = lax.while_loop(lambda st: (st[0] < n_pages) & (st[1] > 0), body, (jnp.int32(0), jnp.int32(1)))

    @pl.when(n_done < n_pages)
    def _():
        for cp in page_copies(n_done % 2, n_done):
            cp.wait()

    zg = z_ref[0]
    for h in range(SB_HEADS):
        sl = slice(h * HEAD_DIM, (h + 1) * HEAD_DIM)
        o_ref[0, :, sl] = (acc_ref[h * t:(h + 1) * t, :] * _silu(zg[:, sl])).astype(o_ref.dtype)


def sb_decode(q, new_kv, cache, page_off, page_table, z):
    b, t, _ = q.shape
    n_pages = page_table.shape[1]
    rows = SB_HEADS * t
    tiles_per_key = 2 * SB_HEADS // SUBLANES
    tok = lambda w: pl.BlockSpec((1, t, w), lambda bi, pt: (bi, 0, 0))
    grid_spec = pltpu.PrefetchScalarGridSpec(
        num_scalar_prefetch=1,
        grid=(b,),
        in_specs=[tok(SB_W), tok(2 * SB_W), tok(SB_W), pl.BlockSpec(memory_space=pl.ANY)],
        out_specs=tok(SB_W),
        scratch_shapes=[
            pltpu.VMEM((2, tiles_per_key, PAGE_SIZE, SUBLANES, HEAD_DIM), F32),
            pltpu.SemaphoreType.DMA((2,)),
            pltpu.VMEM((rows, 1), F32),
            pltpu.VMEM((rows, HEAD_DIM), F32),
        ],
    )
    return pl.pallas_call(
        functools.partial(_sb_decode_kernel, t=t, n_pages=n_pages, page_off=page_off),
        grid_spec=grid_spec,
        out_shape=jax.ShapeDtypeStruct((b, t, SB_W), BF16),
        compiler_params=_cparams(("arbitrary",)),
        name="sb_decode",
    )(page_table, q, new_kv, z, cache)


def _online_update(s, mask, m, l):
    sm = jnp.where(mask, s, NEG)
    m_new = jnp.maximum(m, jnp.max(sm, axis=1, keepdims=True))
    alpha = jnp.exp(m - m_new)
    p = jnp.where(mask, jnp.exp(sm - m_new), 0.0)
    l_new = alpha * l + jnp.sum(p, axis=1, keepdims=True)
    return p, alpha, m_new, l_new


def _flash_step(s, v, m_ref, l_ref, acc_ref):
    m_old = m_ref[...]
    m_new = jnp.maximum(m_old, jnp.max(s, axis=1, keepdims=True))
    alpha = jnp.exp(m_old - m_new)
    p = jnp.exp(s - m_new)
    l_ref[...] = alpha * l_ref[...] + jnp.sum(p, axis=1, keepdims=True)
    m_ref[...] = m_new
    acc_ref[...] = alpha * acc_ref[...] + _dot(p.astype(BF16), v)


def _flash_step_t(s_t, v_t, m_ref, l_ref, acc_ref):
    m_old = m_ref[...]
    m_new = jnp.maximum(m_old, jnp.max(s_t, axis=0, keepdims=True))
    alpha = jnp.exp(m_old - m_new)
    p = jnp.exp(s_t - m_new)
    l_ref[...] = alpha * l_ref[...] + jnp.sum(p, axis=0, keepdims=True)
    m_ref[...] = m_new
    acc_ref[...] = alpha * acc_ref[...] + _dot(v_t, p.astype(BF16))


def _pipelined_sweep(n, scores, fold):
    s = scores(0, True)

    def body(c, s):
        s_next = scores(c + 1, False)
        fold(s, c)
        return s_next

    s = lax.fori_loop(0, n - 1, body, s)

    @pl.when(n >= 1)
    def _():
        s_last = scores(n, True)
        fold(s, n - 1)
        fold(s_last, n)

    @pl.when(n == 0)
    def _():
        fold(s, 0)


def _normalise(acc, l):
    return acc / jnp.maximum(l, 1e-30)


def _diff_lambda(lq1_ref, lk1_ref, lq2_ref, lk2_ref, lambda_init):
    d1 = jnp.sum(lq1_ref[...] * lk1_ref[...], axis=1, keepdims=True)
    d2 = jnp.sum(lq2_ref[...] * lk2_ref[...], axis=1, keepdims=True)
    return jnp.exp(d1) - jnp.exp(d2) + lambda_init


def _diff_finish(o, hg, zg, lambda_init):
    ms = jnp.mean(o * o, axis=-1, keepdims=True)
    o = o * lax.rsqrt(ms + EPS) * hg * (1.0 - lambda_init)
    return o * _silu(zg)


def _diff_prompt_kernel(slopes_ref, q_ref, k_ref, v_ref, z_ref, lq1_ref, lk1_ref, lq2_ref, lk2_ref, hg_ref, o_ref,
                        vt_ref, m_ref, l_ref, acc_ref, *, tq, tk, lambda_init):
    h = pl.program_id(1)
    i = pl.program_id(2)
    slope = slopes_ref[h]
    t_total = v_ref.shape[1]

    @pl.when(i == 0)
    def _():
        for c in range(t_total // tk):
            vt_ref[c] = v_ref[0, c * tk:(c + 1) * tk, :].T.astype(BF16)

    q = q_ref[0] * DIFF_HALF ** -0.5
    qs = [q[:, :DIFF_HALF].astype(BF16), q[:, DIFF_HALF:].astype(BF16)]
    rc = _iota((tk, tq), 1) - _iota((tk, tq), 0)
    bias_rc = slope * rc.astype(F32)
    m_ref[...] = jnp.full_like(m_ref, NEG)
    l_ref[...] = jnp.zeros_like(l_ref)
    acc_ref[...] = jnp.zeros_like(acc_ref)

    def scores(j, edge):
        start = pl.multiple_of(j * tk, tk)
        k = k_ref[0, pl.ds(start, tk), :]
        off = (i - j) * tk
        bias = bias_rc + slope * off.astype(F32)
        out = []
        for c in range(2):
            s = _dot_nt(k[:, c * DIFF_HALF:(c + 1) * DIFF_HALF].astype(BF16), qs[c]) - bias
            out.append(jnp.where(rc + off >= 0, s, NEG) if edge else s)
        return tuple(out)

    def fold(s, j):
        vt = vt_ref[j]
        for c in range(2):
            _flash_step_t(s[c], vt, m_ref.at[c], l_ref.at[c], acc_ref.at[c])

    _pipelined_sweep(i, scores, fold)
    lam = _diff_lambda(lq1_ref, lk1_ref, lq2_ref, lk2_ref, lambda_init)
    o_t = _normalise(acc_ref[0], l_ref[0]) - lam * _normalise(acc_ref[1], l_ref[1])
    o_ref[0] = _diff_finish(o_t.T, hg_ref[...], z_ref[0], lambda_init).astype(o_ref.dtype)


def _smem_spec():
    return pl.BlockSpec(memory_space=pltpu.SMEM)


def diff_prompt(q, kv, z, lq1, lk1, lq2, lk2, head_g, lambda_init, *, tq=256, tk=256):
    b, t, _ = q.shape
    tq = min(tq, t)
    tk = min(tk, tq)
    assert tq == tk and t % tq == 0
    hq = pl.BlockSpec((1, tq, DIFF_VDIM), lambda bi, h, i: (bi, i, h))
    vec = lambda w: pl.BlockSpec((1, w), lambda bi, h, i: (0, 0))
    return pl.pallas_call(
        functools.partial(_diff_prompt_kernel, tq=tq, tk=tk, lambda_init=lambda_init),
        grid=(b, DIFF_HEADS, t // tq),
        in_specs=[
            _smem_spec(),
            hq,
            pl.BlockSpec((1, t, DIFF_VDIM), lambda bi, h, i: (bi, 0, h)),
            pl.BlockSpec((1, t, DIFF_VDIM), lambda bi, h, i: (bi, 0, DIFF_HEADS + h)),
            hq,
            vec(DIFF_HALF), vec(DIFF_HALF), vec(DIFF_HALF), vec(DIFF_HALF), vec(DIFF_VDIM),
        ],
        out_specs=hq,
        out_shape=jax.ShapeDtypeStruct((b, t, C_W), BF16),
        scratch_shapes=[pltpu.VMEM((t // tk, DIFF_VDIM, tk), BF16),
                        pltpu.VMEM((2, 1, tq), F32), pltpu.VMEM((2, 1, tq), F32),
                        pltpu.VMEM((2, DIFF_VDIM, tq), F32)],
        compiler_params=_cparams(("parallel", "parallel", "arbitrary")),
        name="diff_prompt",
    )(jnp.asarray(_alibi_slopes(DIFF_HEADS)), q, kv, kv, z,
      lq1.reshape(1, -1), lk1.reshape(1, -1), lq2.reshape(1, -1), lk2.reshape(1, -1), head_g.reshape(1, -1))


def _diff_decode_kernel(pt_ref, q_ref, knew_ref, vnew_ref, kpage_ref, vpage_ref, z_ref, slope_ref, lq1_ref, lk1_ref,
                        lq2_ref, lk2_ref, hg_ref, o_ref, qs_ref, bias_ref, biasn_ref, m_ref, l_ref, acc_ref,
                        *, t, n_pages, lambda_init):
    p = pl.program_id(1)
    scale = DIFF_HALF ** -0.5
    n_tiles = DIFF_HEADS // SUBLANES
    hr = SUBLANES * t
    past = n_pages * PAGE_SIZE

    def tile_bias(a, n_keys, causal):
        shape = (hr, n_keys * SUBLANES)
        r, c = _iota(shape, 0), _iota(shape, 1)
        ok = (r // t) == (c % SUBLANES)
        rel = r % t - c // SUBLANES
        if causal:
            ok = ok & (rel >= 0)
        return jnp.where(ok, -slope_ref[a] * rel.astype(F32), NEG)

    def attend(a, get_k, v, bias):
        s = jnp.concatenate([_dot_nt(qs_ref[c, a], get_k(c).astype(BF16)) + bias for c in range(2)], axis=0)
        m_old = m_ref[a]
        m_new = jnp.maximum(m_old, jnp.max(s, axis=1, keepdims=True))
        alpha = jnp.exp(m_old - m_new)
        pr = jnp.exp(s - m_new)
        l_ref[a] = alpha * l_ref[a] + jnp.sum(pr, axis=1, keepdims=True)
        m_ref[a] = m_new
        acc_ref[a] = alpha * acc_ref[a] + _dot(pr.astype(BF16), v.astype(BF16))

    @pl.when(p == 0)
    def _():
        q = q_ref[0] * scale
        for a in range(n_tiles):
            for c in range(2):
                cols = [(a * SUBLANES + hl) * DIFF_VDIM + c * DIFF_HALF for hl in range(SUBLANES)]
                qs_ref[c, a] = jnp.concatenate([q[:, o:o + DIFF_HALF] for o in cols], axis=0).astype(BF16)
            bias_ref[a] = tile_bias(a, PAGE_SIZE, False)
            biasn_ref[a] = tile_bias(a, t, True)
        m_ref[...] = jnp.full_like(m_ref, NEG)
        l_ref[...] = jnp.zeros_like(l_ref)
        acc_ref[...] = jnp.zeros_like(acc_ref)

    @pl.when(p < n_pages)
    def _():
        base = (past - p * PAGE_SIZE).astype(F32)
        for a in range(n_tiles):
            rows = slice(a * SUBLANES, (a + 1) * SUBLANES)
            get_k = lambda c: kpage_ref[:, rows, c * DIFF_HALF:(c + 1) * DIFF_HALF].reshape(
                PAGE_SIZE * SUBLANES, DIFF_HALF)
            v = vpage_ref[:, rows, :].reshape(PAGE_SIZE * SUBLANES, DIFF_VDIM)
            attend(a, get_k, v, bias_ref[a] - slope_ref[a] * base)

    @pl.when(p == n_pages)
    def _():
        lam = _diff_lambda(lq1_ref, lk1_ref, lq2_ref, lk2_ref, lambda_init)
        zg = z_ref[0]
        hg = hg_ref[...]
        for a in range(n_tiles):
            rows = slice(a * SUBLANES, (a + 1) * SUBLANES)
            get_k = lambda c: knew_ref[0, :, rows, c * DIFF_HALF:(c + 1) * DIFF_HALF].reshape(t * SUBLANES, DIFF_HALF)
            attend(a, get_k, vnew_ref[0, :, rows, :].reshape(t * SUBLANES, DIFF_VDIM), biasn_ref[a])
            o = _normalise(acc_ref[a], l_ref[a])
            for hl in range(SUBLANES):
                r1 = slice(hl * t, (hl + 1) * t)
                r2 = slice(hr + hl * t, hr + (hl + 1) * t)
                sl = slice((a * SUBLANES + hl) * DIFF_VDIM, (a * SUBLANES + hl + 1) * DIFF_VDIM)
                o_ref[0, :, sl] = _diff_finish(o[r1] - lam * o[r2], hg, zg[:, sl], lambda_init).astype(o_ref.dtype)


def diff_decode(q, new_kv, cache, page_off, page_table, z, lq1, lk1, lq2, lk2, head_g, lambda_init):
    b, t, _ = q.shape
    n_pages = page_table.shape[1]
    n_tiles = DIFF_HEADS // SUBLANES
    hr = SUBLANES * t
    slopes = np.repeat(_alibi_slopes(DIFF_HEADS), t).reshape(n_tiles, hr, 1)
    tok = lambda w: pl.BlockSpec((1, t, w), lambda bi, p, pt: (bi, 0, 0))
    vec = lambda w: pl.BlockSpec((1, w), lambda bi, p, pt: (0, 0))
    new_spec = lambda kv: pl.BlockSpec((1, t, DIFF_HEADS, DIFF_VDIM), lambda bi, p, pt: (bi, 0, kv, 0))
    page_spec = lambda kv: pl.BlockSpec(
        (None, PAGE_SIZE, DIFF_HEADS, DIFF_VDIM),
        lambda bi, p, pt: (page_off + pt[bi, jnp.minimum(p, n_pages - 1)], 0, kv, 0))
    grid_spec = pltpu.PrefetchScalarGridSpec(
        num_scalar_prefetch=1,
        grid=(b, n_pages + 1),
        in_specs=[
            tok(C_W), new_spec(0), new_spec(1), page_spec(0), page_spec(1), tok(C_W),
            pl.BlockSpec((n_tiles, hr, 1), lambda bi, p, pt: (0, 0, 0)),
            vec(DIFF_HALF), vec(DIFF_HALF), vec(DIFF_HALF), vec(DIFF_HALF), vec(DIFF_VDIM),
        ],
        out_specs=tok(C_W),
        scratch_shapes=[
            pltpu.VMEM((2, n_tiles, hr, DIFF_HALF), BF16),
            pltpu.VMEM((n_tiles, hr, PAGE_SIZE * SUBLANES), F32),
            pltpu.VMEM((n_tiles, hr, t * SUBLANES), F32),
            pltpu.VMEM((n_tiles, 2 * hr, 1), F32),
            pltpu.VMEM((n_tiles, 2 * hr, 1), F32),
            pltpu.VMEM((n_tiles, 2 * hr, DIFF_VDIM), F32),
        ],
    )
    return pl.pallas_call(
        functools.partial(_diff_decode_kernel, t=t, n_pages=n_pages, lambda_init=lambda_init),
        grid_spec=grid_spec,
        out_shape=jax.ShapeDtypeStruct((b, t, C_W), BF16),
        compiler_params=_cparams(("parallel", "arbitrary")),
        name="diff_decode",
    )(page_table, q, new_kv, new_kv, cache, cache, z, jnp.asarray(slopes),
      lq1.reshape(1, -1), lk1.reshape(1, -1), lq2.reshape(1, -1), lk2.reshape(1, -1), head_g.reshape(1, -1))


def _masked_softmax_parts(parts):
    m = None
    for s, mask in parts:
        pm = jnp.max(jnp.where(mask, s, NEG), axis=1, keepdims=True)
        m = pm if m is None else jnp.maximum(m, pm)
    es = [jnp.where(mask, jnp.exp(jnp.where(mask, s, NEG) - m), 0.0) for s, mask in parts]
    den = None
    for e in es:
        d = jnp.sum(e, axis=1, keepdims=True)
        den = d if den is None else den + d
    den = jnp.maximum(den, 1e-30)
    return [e / den for e in es]


def _nsa_prompt_kernel(slopes_ref, q_ref, kc_ref, vc_ref, ks_ref, vs_ref, kw_ref, vw_ref, lg_ref, z_ref,
                       wk_ref, wv_ref, expand_ref, o_ref, kcmp_ref, vcmp_ref, vst_ref, vwt_ref, selbias_ref, m_ref,
                       l_ref, acc_ref, *, tq, t_total):
    g = pl.program_id(1)
    i = pl.program_id(2)
    scale = HEAD_DIM ** -0.5
    nb = t_total // SEL_BLOCK
    rows = NSA_GROUP * tq
    tk = tq

    @pl.when(i == 0)
    def _():
        for src, w_ref, dst in ((kc_ref, wk_ref, kcmp_ref), (vc_ref, wv_ref, vcmp_ref)):
            x = src[0].reshape(nb, SEL_BLOCK, HEAD_DIM)
            w = w_ref[0][None]
            even = jnp.sum(x[:, :CMP_BLOCK, :] * w, axis=1)
            odd = jnp.sum(x[:, CMP_BLOCK:, :] * w, axis=1)
            pad = jnp.zeros((LANES - 2 * nb, HEAD_DIM), F32)
            dst[...] = jnp.concatenate([even, odd, pad], axis=0).astype(BF16)
        for src, dst in ((vs_ref, vst_ref), (vw_ref, vwt_ref)):
            for c in range(t_total // tk):
                dst[c] = src[0, c * tk:(c + 1) * tk, :].T.astype(BF16)

    q = q_ref[0] * scale
    q4 = jnp.concatenate([q[:, zz * HEAD_DIM:(zz + 1) * HEAD_DIM] for zz in range(NSA_GROUP)], axis=0).astype(BF16)
    qpos1 = i * tq + _iota((tq, 1), 0)
    qpos4 = jnp.concatenate([qpos1] * NSA_GROUP, axis=0)
    slope4 = jnp.concatenate(
        [jnp.full((tq, 1), slopes_ref[g * NSA_GROUP + zz], F32) for zz in range(NSA_GROUP)], axis=0)
    col = _iota((1, LANES), 1)

    cidx = jnp.where(col < nb, 2 * col, 2 * (col - nb) + 1)
    dist_c = qpos4 - (cidx * CMP_BLOCK + (CMP_BLOCK - 1))
    s_c = _dot_nt(q4, kcmp_ref[...]) - slope4 * dist_c.astype(F32)
    (p_c,) = _masked_softmax_parts([(s_c, (dist_c >= 0) & (col < 2 * nb))])
    o_cmp = _dot(p_c.astype(BF16), vcmp_ref[...])
    imp = p_c[0:tq]
    for zz in range(1, NSA_GROUP):
        imp = imp + p_c[zz * tq:(zz + 1) * tq]
    pair = imp + pltpu.roll(imp, LANES - nb, 1)

    cur = qpos1 // SEL_BLOCK
    valid = col * SEL_BLOCK <= qpos1
    forced = (col == cur) | (col == 0)
    score = jnp.where(forced, FORCED_SCORE, jnp.where(valid, pair, -1.0))
    score = jnp.where(col < nb, score, -2.0)
    score_t = score.T
    cand = score_t[:nb]
    blk = _iota((nb, 1), 0)
    rank = jnp.zeros((nb, tq), F32)
    for j in range(nb):
        r = score_t[j:j + 1, :]
        ge = jnp.where(r >= cand, 1.0, 0.0)
        gt = jnp.where(r > cand, 1.0, 0.0)
        rank = rank + jnp.where(blk > j, ge, gt)
    sel_t = (rank < float(min(SEL_TOPK, nb))).astype(F32)
    sel_rows = jnp.concatenate([sel_t, jnp.zeros((LANES - nb, tq), F32)], axis=0).astype(BF16)
    selbias_ref[...] = (_dot(expand_ref[...], sel_rows) - 1.0) * (-NEG)

    tile = lambda x: jnp.concatenate([x] * NSA_GROUP, axis=1)
    rc4 = tile(_iota((tk, tq), 1) - _iota((tk, tq), 0))
    slope_row = jnp.concatenate(
        [jnp.full((1, tq), slopes_ref[g * NSA_GROUP + zz], F32) for zz in range(NSA_GROUP)], axis=1)
    bias_rc = slope_row * rc4.astype(F32)

    def reset():
        m_ref[...] = jnp.full_like(m_ref, NEG)
        l_ref[...] = jnp.zeros_like(l_ref)
        acc_ref[...] = jnp.zeros_like(acc_ref)

    def scores(k_ref, j, admit):
        start = pl.multiple_of(j * tk, tk)
        k = k_ref[0, pl.ds(start, tk), :].astype(BF16)
        off = (i - j) * tk
        s = _dot_nt(k, q4) - (bias_rc + slope_row * off.astype(F32))
        return admit(s, start, rc4 + off)

    def slc_admit(edge):
        def admit(s, start, dist):
            s = s + tile(selbias_ref[pl.ds(start, tk), :])
            return jnp.where(dist >= 0, s, NEG) if edge else s
        return admit

    reset()
    _pipelined_sweep(i, lambda c, edge: scores(ks_ref, c, slc_admit(edge)),
                     lambda s, c: _flash_step_t(s, vst_ref[c], m_ref, l_ref, acc_ref))
    o_slc_t = _normalise(acc_ref[...], l_ref[...])

    def win_admit(edge):
        return (lambda s, start, dist: jnp.where((dist >= 0) & (dist < WINDOW), s, NEG)) if edge else (
            lambda s, start, dist: s)

    reset()
    _pipelined_sweep(jnp.minimum(i, WINDOW // tk), lambda c, edge: scores(kw_ref, i - c, win_admit(edge)),
                     lambda s, c: _flash_step_t(s, vwt_ref[i - c], m_ref, l_ref, acc_ref))
    o_win_t = _normalise(acc_ref[...], l_ref[...])

    gates = pltpu.roll(_sigmoid(lg_ref[0]), (LANES - 3 * NSA_GROUP * g) % LANES, 1)
    zg = z_ref[0]
    for zz in range(NSA_GROUP):
        r = slice(zz * tq, (zz + 1) * tq)
        mix = (gates[:, 3 * zz:3 * zz + 1] * o_cmp[r] + gates[:, 3 * zz + 1:3 * zz + 2] * o_slc_t[:, r].T
               + gates[:, 3 * zz + 2:3 * zz + 3] * o_win_t[:, r].T)
        sl = slice(zz * HEAD_DIM, (zz + 1) * HEAD_DIM)
        o_ref[0, :, sl] = (mix * _silu(zg[:, sl])).astype(o_ref.dtype)


def _cmp_weight_rows(w):
    return jnp.broadcast_to(w.T[:, :, None], (NSA_KV_HEADS, CMP_BLOCK, HEAD_DIM)).astype(F32)


def nsa_prompt(q, nskv, win, logits, z, cmp_wk, cmp_wv, *, tq=128):
    b, t, _ = q.shape
    nb = t // SEL_BLOCK
    assert t % tq == 0 and 2 * nb <= LANES and WINDOW % tq == 0
    gw = NSA_GROUP * HEAD_DIM
    expand = ((np.arange(t)[:, None] // SEL_BLOCK) == np.arange(LANES)[None, :]).astype(np.float32)
    kvspec = lambda kind: pl.BlockSpec((1, t, HEAD_DIM), lambda bi, g, i: (bi, 0, kind * NSA_KV_HEADS + g))
    qspec = pl.BlockSpec((1, tq, gw), lambda bi, g, i: (bi, i, g))
    wspec = pl.BlockSpec((1, CMP_BLOCK, HEAD_DIM), lambda bi, g, i: (g, 0, 0))
    return pl.pallas_call(
        functools.partial(_nsa_prompt_kernel, tq=tq, t_total=t),
        grid=(b, NSA_KV_HEADS, t // tq),
        in_specs=[
            _smem_spec(),
            qspec,
            kvspec(0), kvspec(1), kvspec(2), kvspec(3),
            kvspec(0), kvspec(1),
            pl.BlockSpec((1, tq, LANES), lambda bi, g, i: (bi, i, 0)),
            qspec,
            wspec, wspec,
            pl.BlockSpec((t, LANES), lambda bi, g, i: (0, 0)),
        ],
        out_specs=qspec,
        out_shape=jax.ShapeDtypeStruct((b, t, NSA_W), BF16),
        scratch_shapes=[
            pltpu.VMEM((LANES, HEAD_DIM), BF16),
            pltpu.VMEM((LANES, HEAD_DIM), BF16),
            pltpu.VMEM((t // tq, HEAD_DIM, tq), BF16),
            pltpu.VMEM((t // tq, HEAD_DIM, tq), BF16),
            pltpu.VMEM((t, tq), F32),
            pltpu.VMEM((1, NSA_GROUP * tq), F32),
            pltpu.VMEM((1, NSA_GROUP * tq), F32),
            pltpu.VMEM((HEAD_DIM, NSA_GROUP * tq), F32),
        ],
        compiler_params=_cparams(("parallel", "parallel", "arbitrary")),
        name="nsa_prompt",
    )(jnp.asarray(_alibi_slopes(NSA_HEADS)), q, nskv, nskv, nskv, nskv, win, win, logits, z,
      _cmp_weight_rows(cmp_wk), _cmp_weight_rows(cmp_wv), jnp.asarray(expand, BF16))


NSA_PAGES_PER_STEP = 4


def _nsa_compress_kernel(pt_ref, *refs):
    page_refs, w_ref, even_ref, odd_ref = refs[:-3], refs[-3], refs[-2], refs[-1]
    n_cmp = PAGE_SIZE // CMP_BLOCK
    w = w_ref[...][None]
    evens, odds = [], []
    for page_ref in page_refs:
        x = page_ref[...].reshape(n_cmp, CMP_BLOCK, 2 * NSA_KV_HEADS, HEAD_DIM)
        c = jnp.sum(x * w, axis=1)
        evens += [c[r:r + 1] for r in range(0, n_cmp, 2)]
        odds += [c[r:r + 1] for r in range(1, n_cmp, 2)]
    even_ref[0] = jnp.concatenate(evens, axis=0)
    odd_ref[0] = jnp.concatenate(odds, axis=0)


def nsa_compress_pages(cache, page_off, page_table, cmp_wk, cmp_wv):
    b, n_pages = page_table.shape
    pps = math.gcd(n_pages, NSA_PAGES_PER_STEP)
    half_blocks = PAGE_SIZE // CMP_BLOCK // 2
    w = jnp.concatenate([cmp_wk, cmp_wv], axis=1).astype(F32)
    w = jnp.broadcast_to(w[:, :, None], (CMP_BLOCK, 2 * NSA_KV_HEADS, HEAD_DIM))
    cmp_spec = pl.BlockSpec((1, pps * half_blocks, 2 * NSA_KV_HEADS, HEAD_DIM), lambda bi, p, pt: (bi, p, 0, 0))
    page_spec = lambda k: pl.BlockSpec((None, PAGE_SIZE, 2 * NSA_KV_HEADS, HEAD_DIM),
                                       lambda bi, p, pt: (page_off + pt[bi, p * pps + k], 0, 0, 0))
    grid_spec = pltpu.PrefetchScalarGridSpec(
        num_scalar_prefetch=1,
        grid=(b, n_pages // pps),
        in_specs=[page_spec(k) for k in range(pps)]
        + [pl.BlockSpec((CMP_BLOCK, 2 * NSA_KV_HEADS, HEAD_DIM), lambda bi, p, pt: (0, 0, 0))],
        out_specs=[cmp_spec, cmp_spec],
    )
    return pl.pallas_call(
        _nsa_compress_kernel,
        grid_spec=grid_spec,
        out_shape=[jax.ShapeDtypeStruct((b, n_pages * half_blocks, 2 * NSA_KV_HEADS, HEAD_DIM), F32)] * 2,
        compiler_params=_cparams(("parallel", "parallel")),
        name="nsa_compress_pages",
    )(page_table, *([cache] * pps), w)


def _nsa_decode_kernel(pt_ref, q_ref, even_ref, odd_ref, *refs, t, n_pages, pps):
    page_refs = refs[:pps]
    (new_ref, wst_ref, wnew_ref, lg_ref, z_ref, slope_ref, o_ref, qg_ref, sel_ref, ocmp_ref, m_ref, l_ref,
     acc_ref) = refs[pps:]
    p = pl.program_id(1)
    n_steps = n_pages // pps
    scale = HEAD_DIM ** -0.5
    rows = NSA_HEADS * t
    grp_rows = NSA_GROUP * t
    past = n_pages * PAGE_SIZE
    n_blk = past // SEL_BLOCK
    half = n_blk
    tq = _iota((rows, 1), 0) % t
    qpos = past + tq
    slope = slope_ref[...]
    col = _iota((1, PAGE_SIZE), 1)

    def scores(get_k):
        return jnp.concatenate([_dot_nt(qg_ref[gg], get_k(gg).astype(BF16)) for gg in range(NSA_KV_HEADS)],
                               axis=0) * scale

    def group_pv(pr, get_v):
        return jnp.concatenate([_dot(pr[gg * grp_rows:(gg + 1) * grp_rows], get_v(gg).astype(BF16))
                                for gg in range(NSA_KV_HEADS)], axis=0)

    def rows_kv(ref3):
        return (lambda gg: ref3[:, gg, :]), (lambda gg: ref3[:, NSA_KV_HEADS + gg, :])

    def lanes_kv(x, first_block):
        blk = lambda i: x[:, (first_block + i) * HEAD_DIM:(first_block + i + 1) * HEAD_DIM]
        return (lambda gg: blk(gg)), (lambda gg: blk(NSA_KV_HEADS + gg))

    @pl.when(p == 0)
    def _():
        q = q_ref[0]
        for gg in range(NSA_KV_HEADS):
            heads = range(gg * NSA_GROUP, (gg + 1) * NSA_GROUP)
            qg_ref[gg] = jnp.concatenate([q[:, h * HEAD_DIM:(h + 1) * HEAD_DIM] for h in heads], axis=0).astype(BF16)
        parts, vals = [], []
        for par, cmp_ref in enumerate((even_ref, odd_ref)):
            get_k, get_v = rows_kv(cmp_ref.at[0])
            vals.append(get_v)
            c_end = (2 * _iota((1, half), 1) + par) * CMP_BLOCK + (CMP_BLOCK - 1)
            dist = qpos - c_end
            parts.append((scores(get_k) - slope * dist.astype(F32), dist >= 0))
        p_e, p_o = _masked_softmax_parts(parts)
        ocmp_ref[...] = group_pv(p_e.astype(BF16), vals[0]) + group_pv(p_o.astype(BF16), vals[1])
        pe = p_e + p_o
        blkcol = _iota((1, n_blk), 1)
        picked = []
        for gg in range(NSA_KV_HEADS):
            imp = pe[gg * grp_rows:gg * grp_rows + t]
            for zz in range(1, NSA_GROUP):
                imp = imp + pe[gg * grp_rows + zz * t:gg * grp_rows + (zz + 1) * t]
            sc = jnp.where(blkcol == 0, -1.0, imp)
            sel = (blkcol == 0)
            for _ in range(min(SEL_TOPK, n_blk + 1) - 2):
                mx = jnp.max(sc, axis=1, keepdims=True)
                first = jnp.min(jnp.where(sc == mx, blkcol, n_blk), axis=1, keepdims=True)
                hit = blkcol == first
                sel = sel | hit
                sc = jnp.where(hit, -1.0, sc)
            self32 = sel.astype(F32)
            picked.extend([self32] * NSA_GROUP)
        sel_ref[...] = jnp.concatenate(picked, axis=0).astype(BF16)
        m_ref[...] = jnp.full_like(m_ref, NEG)
        l_ref[...] = jnp.zeros_like(l_ref)
        acc_ref[...] = jnp.zeros_like(acc_ref)

    def attend(get_k, get_v, dist, mask):
        s = scores(get_k) - slope * dist.astype(F32)
        pr, alpha, m_new, l_new = _online_update(s, mask, m_ref[...], l_ref[...])
        m_ref[...] = m_new
        l_ref[...] = l_new
        acc_ref[...] = alpha * acc_ref[...] + group_pv(pr.astype(BF16), get_v)

    @pl.when(p < n_steps)
    def _():
        keys = pps * PAGE_SIZE
        blocks_per_step = keys // SEL_BLOCK
        expand = (_iota((n_blk, keys), 0) == blocks_per_step * p + _iota((n_blk, keys), 1) // SEL_BLOCK)
        chosen = _dot(sel_ref[...], expand.astype(BF16)) > 0.5
        dist = qpos - (p * keys + _iota((1, keys), 1))
        getters = [rows_kv(page_ref) for page_ref in page_refs]
        get_k = lambda gg: jnp.concatenate([gk(gg) for gk, _ in getters], axis=0)
        get_v = lambda gg: jnp.concatenate([gv(gg) for _, gv in getters], axis=0)
        attend(get_k, get_v, dist, chosen & (dist >= 0))

    @pl.when(p == n_steps)
    def _():
        new = _pad_rows(new_ref[0], PAGE_SIZE)
        dist = qpos - (past + col)
        get_k, get_v = lanes_kv(new, 2 * NSA_KV_HEADS)
        attend(get_k, get_v, dist, (dist >= 0) & (col < t))
        o_slc = _normalise(acc_ref[...], l_ref[...])
        n_state = wst_ref.shape[1]
        dist_s = qpos - (past - n_state + _iota((1, n_state), 1))
        ks_state, vs_state = rows_kv(wst_ref.at[0])
        ks_new, vs_new = lanes_kv(_pad_rows(wnew_ref[0], PAGE_SIZE), 0)
        p_s, p_n = _masked_softmax_parts([
            (scores(ks_state) - slope * dist_s.astype(F32), (dist_s >= 0) & (dist_s < WINDOW)),
            (scores(ks_new) - slope * dist.astype(F32), (dist >= 0) & (dist < WINDOW) & (col < t)),
        ])
        o_win = group_pv(p_s.astype(BF16), vs_state) + group_pv(p_n.astype(BF16), vs_new)
        o_cmp = ocmp_ref[...]
        gates = _sigmoid(lg_ref[0])
        zg = z_ref[0]
        for h in range(NSA_HEADS):
            r = slice(h * t, (h + 1) * t)
            mix = (gates[:, 3 * h:3 * h + 1] * o_cmp[r] + gates[:, 3 * h + 1:3 * h + 2] * o_slc[r]
                   + gates[:, 3 * h + 2:3 * h + 3] * o_win[r])
            sl = slice(h * HEAD_DIM, (h + 1) * HEAD_DIM)
            o_ref[0, :, sl] = (mix * _silu(zg[:, sl])).astype(o_ref.dtype)


def nsa_decode(q, cmp_even, cmp_odd, cache, page_off, page_table, new_nskv, win_state, win_new, logits, z):
    b, t, _ = q.shape
    n_pages = page_table.shape[1]
    past = n_pages * PAGE_SIZE
    n_blk = past // SEL_BLOCK
    n_state = win_state.shape[1]
    assert past % SEL_BLOCK == 0 and t < CMP_BLOCK and n_blk + 1 >= SEL_TOPK and n_state >= WINDOW - 1
    rows = NSA_HEADS * t
    slopes = np.repeat(_alibi_slopes(NSA_HEADS), t).reshape(rows, 1)
    pps = math.gcd(n_pages, NSA_PAGES_PER_STEP)
    n_steps = n_pages // pps
    tok = lambda w: pl.BlockSpec((1, t, w), lambda bi, p, pt: (bi, 0, 0))
    page_spec = lambda k: pl.BlockSpec(
        (None, PAGE_SIZE, 2 * NSA_KV_HEADS, HEAD_DIM),
        lambda bi, p, pt: (page_off + pt[bi, jnp.minimum(p, n_steps - 1) * pps + k], 0, 1, 0))
    grid_spec = pltpu.PrefetchScalarGridSpec(
        num_scalar_prefetch=1,
        grid=(b, n_steps + 1),
        in_specs=[
            tok(NSA_W),
            pl.BlockSpec((1, n_blk, 2 * NSA_KV_HEADS, HEAD_DIM), lambda bi, p, pt: (bi, 0, 0, 0)),
            pl.BlockSpec((1, n_blk, 2 * NSA_KV_HEADS, HEAD_DIM), lambda bi, p, pt: (bi, 0, 0, 0)),
            *[page_spec(k) for k in range(pps)],
            tok(4 * NSA_KV_W),
            pl.BlockSpec((1, n_state, 2 * NSA_KV_HEADS, HEAD_DIM), lambda bi, p, pt: (bi, 0, 0, 0)),
            tok(2 * NSA_KV_W),
            tok(LANES),
            tok(NSA_W),
            pl.BlockSpec((rows, 1), lambda bi, p, pt: (0, 0)),
        ],
        out_specs=tok(NSA_W),
        scratch_shapes=[
            pltpu.VMEM((NSA_KV_HEADS, NSA_GROUP * t, HEAD_DIM), BF16),
            pltpu.VMEM((rows, n_blk), BF16),
            pltpu.VMEM((rows, HEAD_DIM), F32),
            pltpu.VMEM((rows, 1), F32),
            pltpu.VMEM((rows, 1), F32),
            pltpu.VMEM((rows, HEAD_DIM), F32),
        ],
    )
    return pl.pallas_call(
        functools.partial(_nsa_decode_kernel, t=t, n_pages=n_pages, pps=pps),
        grid_spec=grid_spec,
        out_shape=jax.ShapeDtypeStruct((b, t, NSA_W), BF16),
        compiler_params=_cparams(("parallel", "arbitrary")),
        name="nsa_decode",
    )(page_table, q, cmp_even, cmp_odd, *([cache] * pps), new_nskv, win_state, win_new, logits, z, jnp.asarray(slopes))


def _ab_projections(hn, w_in):
    o = np.cumsum([0, SB_W, 2 * SB_W, SB_W, NSA_W, 4 * NSA_KV_W, 2 * NSA_KV_W, 3 * NSA_HEADS, NSA_W])
    names = ("sbq", "sbkv", "sbz", "nsq", "nskv", "win")
    proj = {name: matmul([hn], w_in, col0=int(o[c]), n=int(o[c + 1] - o[c])) for c, name in enumerate(names)}
    proj["logits"] = matmul([hn], jnp.pad(w_in[:, o[6]:o[7]], ((0, 0), (0, LANES - 3 * NSA_HEADS))))
    proj["nsz"] = matmul([hn], w_in[:, o[7]:o[8]])
    return proj


def _c_projections(hn, w_in):
    return dict(q=matmul([hn], w_in, col0=0, n=C_W), kv=matmul([hn], w_in, col0=C_W, n=2 * C_W),
                z=matmul([hn], w_in, col0=3 * C_W, n=C_W))


def kernel(x_prompt, x_sample, p_prompt, p_sample, cache_sb_kv, cache_nsa_kv, state_nsa_win_kv, cache_diff_kv,
           page_table, norm_g, w_in_ab, nsa_cmp_wk, nsa_cmp_wv, w_out_ab, w_in_c, diff_lq1, diff_lk1, diff_lq2,
           diff_lk2, diff_head_g, w_out_c, ple_norm_g, w_ple_gate, w_ple_proj, final_norm_g):
    bp, tp, d = x_prompt.shape
    bs, ts, _ = x_sample.shape
    depth = norm_g.shape[0]
    n_phys = cache_sb_kv.shape[1]

    def run(x, p_emb, sample):
        b, t, _ = x.shape
        h = x.reshape(b * t, d)
        sb_rows, nsa_rows, win_rows, diff_rows = [], [], [], []
        for i in range(depth):
            j = i // 2
            hn = rmsnorm(h, norm_g[i], BF16)
            if i % 2 == 0:
                proj = {name: y.reshape(b, t, -1) for name, y in _ab_projections(hn, w_in_ab[j]).items()}
                win_new = proj["win"].reshape(b, t, 2, NSA_KV_HEADS, HEAD_DIM)
                if sample:
                    sb_mixed = sb_decode(proj["sbq"], proj["sbkv"],
                                         cache_sb_kv.reshape(-1, PAGE_SIZE, 2 * SB_HEADS, HEAD_DIM), j * n_phys,
                                         page_table, proj["sbz"])
                    nsa_cache = cache_nsa_kv.reshape(-1, PAGE_SIZE, 4 * NSA_KV_HEADS, HEAD_DIM)
                    cmp_even, cmp_odd = nsa_compress_pages(nsa_cache, j * n_phys, page_table, nsa_cmp_wk[j],
                                                           nsa_cmp_wv[j])
                    win_rows_in = state_nsa_win_kv[j].reshape(b, -1, 2 * NSA_KV_HEADS, HEAD_DIM)
                    ns_mixed = nsa_decode(proj["nsq"], cmp_even, cmp_odd, nsa_cache, j * n_phys, page_table,
                                          proj["nskv"], win_rows_in, proj["win"], proj["logits"], proj["nsz"])
                    win_all = jnp.concatenate([state_nsa_win_kv[j], win_new], axis=1)
                else:
                    sb_mixed = sb_prompt(proj["sbq"], proj["sbkv"], proj["sbz"])
                    ns_mixed = nsa_prompt(proj["nsq"], proj["nskv"], proj["win"], proj["logits"], proj["nsz"],
                                          nsa_cmp_wk[j], nsa_cmp_wv[j])
                    win_all = win_new
                h = matmul([sb_mixed.reshape(b * t, -1), ns_mixed.reshape(b * t, -1)], w_out_ab[j], residual=h)
                sb_rows.append(proj["sbkv"].reshape(b, t, 2, SB_HEADS, HEAD_DIM))
                nsa_rows.append(proj["nskv"].reshape(b, t, 4, NSA_KV_HEADS, HEAD_DIM))
                keep = min(WINDOW, win_all.shape[1])
                win_rows.append(win_all[:, win_all.shape[1] - keep:])
            else:
                lambda_init = 0.8 - 0.6 * math.exp(-0.3 * i)
                proj = {name: y.reshape(b, t, -1) for name, y in _c_projections(hn, w_in_c[j]).items()}
                lam_args = (diff_lq1[j], diff_lk1[j], diff_lq2[j], diff_lk2[j], diff_head_g[j], lambda_init)
                if sample:
                    mixed = diff_decode(proj["q"], proj["kv"].reshape(b, t, 2 * DIFF_HEADS, DIFF_VDIM),
                                        cache_diff_kv.reshape(-1, PAGE_SIZE, 2 * DIFF_HEADS, DIFF_VDIM), j * n_phys,
                                        page_table, proj["z"], *lam_args)
                else:
                    mixed = diff_prompt(proj["q"], proj["kv"], proj["z"], *lam_args)
                h = matmul([mixed.reshape(b * t, -1)], w_out_c[j], residual=h)
                diff_rows.append(proj["kv"].reshape(b, t, 2, DIFF_HEADS, DIFF_VDIM))
            hn2 = rmsnorm(h, ple_norm_g[i], BF16)
            h = ple(hn2, w_ple_gate[i], p_emb[i].reshape(b * t, -1).astype(BF16), w_ple_proj[i], h)
        y = rmsnorm(h, final_norm_g, F32).reshape(b, t, d)
        return y, jnp.stack(sb_rows), jnp.stack(nsa_rows), jnp.stack(win_rows), jnp.stack(diff_rows)

    y_p, sb_p, nsa_p, win_p, diff_p = run(x_prompt, p_prompt, False)
    y_s, sb_s, nsa_s, win_s, diff_s = run(x_sample, p_sample, True)
    return (y_p, y_s, sb_p, sb_s, nsa_p, nsa_s, win_p, win_s, diff_p, diff_s)
```

```python
import functools
import math

import jax
import jax.numpy as jnp
import numpy as np
from jax import lax
from jax.experimental import pallas as pl
from jax.experimental.pallas import tpu as pltpu

F32 = jnp.float32
BF16 = jnp.bfloat16

HEAD_DIM = 128
SB_HEADS = 16
NSA_HEADS = 16
NSA_KV_HEADS = 4
NSA_GROUP = NSA_HEADS // NSA_KV_HEADS
CMP_BLOCK = 32
SEL_BLOCK = 64
SEL_TOPK = 16
WINDOW = 512
DIFF_HEADS = 16
DIFF_HALF = 128
DIFF_VDIM = 2 * DIFF_HALF
PAGE_SIZE = 128
EPS = 1e-6
NEG = -1e30
FORCED_SCORE = 1e4

SB_W = SB_HEADS * HEAD_DIM
NSA_W = NSA_HEADS * HEAD_DIM
NSA_KV_W = NSA_KV_HEADS * HEAD_DIM
C_W = DIFF_HEADS * DIFF_VDIM
LANES = 128
SUBLANES = 8
EXP_UNDERFLOW = -110.0
VMEM_LIMIT = 56 * 1024 * 1024


def _alibi_slopes(n):
    return np.asarray(2.0 ** (-8.0 * np.arange(1, n + 1) / n), dtype=np.float32)


def _cparams(sem):
    return pltpu.CompilerParams(dimension_semantics=sem, vmem_limit_bytes=VMEM_LIMIT)


def _dot(a, b):
    return jnp.dot(a, b, preferred_element_type=F32)


def _dot_nt(a, b):
    return lax.dot_general(a, b, (((1,), (1,)), ((), ())), preferred_element_type=F32)


def _silu(z):
    return z * (1.0 / (1.0 + jnp.exp(-z)))


def _sigmoid(z):
    return 1.0 / (1.0 + jnp.exp(-z))


def _iota(shape, dim):
    return lax.broadcasted_iota(jnp.int32, shape, dim)


def _rmsnorm_kernel(x_ref, g_ref, o_ref):
    x = x_ref[...]
    ms = jnp.mean(x * x, axis=-1, keepdims=True)
    o_ref[...] = (x * lax.rsqrt(ms + EPS) * g_ref[...]).astype(o_ref.dtype)


def rmsnorm(x, g, out_dtype):
    m, d = x.shape
    tm = min(m, 256)
    return pl.pallas_call(
        _rmsnorm_kernel,
        grid=(m // tm,),
        in_specs=[pl.BlockSpec((tm, d), lambda i: (i, 0)), pl.BlockSpec((1, d), lambda i: (0, 0))],
        out_specs=pl.BlockSpec((tm, d), lambda i: (i, 0)),
        out_shape=jax.ShapeDtypeStruct((m, d), out_dtype),
        compiler_params=_cparams(("parallel",)),
        name="rmsnorm",
    )(x, g.reshape(1, d))


def _mm_kernel(*refs, n_parts, has_residual):
    a_refs, w_refs = refs[:n_parts], refs[n_parts:2 * n_parts]
    r_ref = refs[2 * n_parts] if has_residual else None
    o_ref = refs[2 * n_parts + has_residual]
    wb_refs = refs[2 * n_parts + has_residual + 1:]

    @pl.when(pl.program_id(1) == 0)
    def _():
        for w_ref, wb_ref in zip(w_refs, wb_refs):
            wb_ref[...] = w_ref[...].astype(BF16)

    acc = _dot(a_refs[0][...], wb_refs[0][...])
    for a_ref, wb_ref in zip(a_refs[1:], wb_refs[1:]):
        acc = acc + _dot(a_ref[...], wb_ref[...])
    if has_residual:
        acc = r_ref[...] + acc
    o_ref[...] = acc


def matmul(a_parts, w, *, col0=0, n=None, residual=None):
    m, kp = a_parts[0].shape
    n_parts = len(a_parts)
    assert all(a.shape == (m, kp) for a in a_parts) and w.shape[0] == n_parts * kp
    n = w.shape[1] - col0 if n is None else n
    tm = min(m, 1024)
    tn = min(n, 512)
    assert m % tm == 0 and n % tn == 0 and col0 % tn == 0
    in_specs = [pl.BlockSpec((tm, kp), lambda j, i: (i, 0)) for _ in a_parts]
    in_specs += [pl.BlockSpec((kp, tn), lambda j, i, part=part: (part, col0 // tn + j)) for part in range(n_parts)]
    args = list(a_parts) + [w] * n_parts
    if residual is not None:
        in_specs.append(pl.BlockSpec((tm, tn), lambda j, i: (i, j)))
        args.append(residual)
    return pl.pallas_call(
        functools.partial(_mm_kernel, n_parts=n_parts, has_residual=residual is not None),
        grid=(n // tn, m // tm),
        in_specs=in_specs,
        out_specs=pl.BlockSpec((tm, tn), lambda j, i: (i, j)),
        out_shape=jax.ShapeDtypeStruct((m, n), F32),
        scratch_shapes=[pltpu.VMEM((kp, tn), BF16) for _ in a_parts],
        compiler_params=_cparams(("parallel", "arbitrary")),
        name="matmul",
    )(*args)


def _ple_kernel(hn_ref, wg_ref, p_ref, wp_ref, h_ref, o_ref, wgb_ref):
    @pl.when(pl.program_id(1) == 0)
    def _():
        wgb_ref[...] = wg_ref[...].astype(BF16)

    gate = _sigmoid(_dot(hn_ref[...], wgb_ref[...]))
    proj = _dot(p_ref[...], wp_ref[...].astype(BF16))
    o_ref[...] = h_ref[...] + gate * proj


def ple(hn, wg, p, wp, h, layer):
    m, d = hn.shape
    n = wg.shape[2]
    pd = p.shape[1]
    tm = min(m, 1024)
    tn = min(n, 512)
    return pl.pallas_call(
        _ple_kernel,
        grid=(n // tn, m // tm),
        in_specs=[
            pl.BlockSpec((tm, d), lambda j, i: (i, 0)),
            pl.BlockSpec((None, d, tn), lambda j, i: (layer, 0, j)),
            pl.BlockSpec((tm, pd), lambda j, i: (i, 0)),
            pl.BlockSpec((None, pd, tn), lambda j, i: (layer, 0, j)),
            pl.BlockSpec((tm, tn), lambda j, i: (i, j)),
        ],
        out_specs=pl.BlockSpec((tm, tn), lambda j, i: (i, j)),
        out_shape=jax.ShapeDtypeStruct((m, n), F32),
        scratch_shapes=[pltpu.VMEM((d, tn), BF16)],
        compiler_params=_cparams(("parallel", "arbitrary")),
        name="ple",
    )(hn, wg, p, wp, h)


def _suffix_sum_matrix(c):
    return (_iota((c, c), 0) >= _iota((c, c), 1)).astype(BF16)


def _sb_prompt_kernel(q_ref, k_ref, v_ref, z_ref, o_ref, carry_ref, acc_ref, *, tq, tk):
    i = pl.program_id(2)
    qb = (q_ref[0] * HEAD_DIM ** -0.5).astype(BF16)
    incl_mat = _suffix_sum_matrix(tk)
    rc = _iota((tq, tk), 0) - _iota((tq, tk), 1)
    carry_ref[...] = jnp.zeros_like(carry_ref)
    acc_ref[...] = jnp.zeros_like(acc_ref)

    def chunk(j, masked):
        start = pl.multiple_of(j * tk, tk)
        k = k_ref[0, pl.ds(start, tk), :].astype(BF16)
        v = v_ref[0, pl.ds(start, tk), :].astype(BF16)
        z = _dot_nt(qb, k)
        lb = jnp.minimum(z, 0.0) - jnp.log(1.0 + jnp.exp(-jnp.abs(z)))
        lk = lb - z
        if masked:
            mask = rc > j * tk - i * tq
            lb = jnp.where(mask, lb, 0.0)
            lk = jnp.where(mask, lk, 0.0)
        hi = lk.astype(BF16)
        lo = (lk - hi.astype(F32)).astype(BF16)
        incl = _dot(hi, incl_mat) + _dot(lo, incl_mat)
        carry = carry_ref[...]
        a = jnp.exp(lb + (incl - lk + carry))
        if masked:
            a = jnp.where(mask, a, 0.0)
        acc_ref[...] += _dot(a.astype(BF16), v)
        carry_ref[...] = carry + incl[:, 0:1]

    n_diag = tq // tk
    top = (i + 1) * n_diag - 1
    for d in range(n_diag):
        chunk(top - d, True)

    def body(state):
        j, _ = state
        chunk(j, False)
        live = jnp.max(carry_ref[...]) > EXP_UNDERFLOW
        return j - 1, live.astype(jnp.int32)

    lax.while_loop(lambda st: (st[0] >= 0) & (st[1] > 0), body, (top - n_diag, jnp.int32(1)))
    o_ref[0] = (acc_ref[...] * _silu(z_ref[0])).astype(o_ref.dtype)


def sb_prompt(q, kv, z, *, tq=512, tk=256):
    b, t, _ = q.shape
    tq = min(tq, t)
    tk = min(tk, tq)
    hq = pl.BlockSpec((1, tq, HEAD_DIM), lambda bi, h, i: (bi, i, h))
    return pl.pallas_call(
        functools.partial(_sb_prompt_kernel, tq=tq, tk=tk),
        grid=(b, SB_HEADS, t // tq),
        in_specs=[
            hq,
            pl.BlockSpec((1, t, HEAD_DIM), lambda bi, h, i: (bi, 0, h)),
            pl.BlockSpec((1, t, HEAD_DIM), lambda bi, h, i: (bi, 0, SB_HEADS + h)),
            hq,
        ],
        out_specs=hq,
        out_shape=jax.ShapeDtypeStruct((b, t, SB_W), BF16),
        scratch_shapes=[pltpu.VMEM((tq, 1), F32), pltpu.VMEM((tq, HEAD_DIM), F32)],
        compiler_params=_cparams(("parallel", "parallel", "arbitrary")),
        name="sb_prompt",
    )(q, kv, kv, z)


def _pad_rows(x, rows):
    return jnp.concatenate([x, jnp.zeros((rows - x.shape[0], x.shape[1]), x.dtype)], axis=0)


def _sb_decode_kernel(pt_ref, q_ref, new_ref, z_ref, cache_ref, o_ref, buf_ref, sem_ref, carry_ref, acc_ref,
                      *, t, n_pages, page_off):
    b = pl.program_id(0)
    scale = HEAD_DIM ** -0.5
    rows = SB_HEADS * t
    tiles_per_key = 2 * SB_HEADS // SUBLANES
    tq = _iota((rows, 1), 0) % t
    col = _iota((1, PAGE_SIZE), 1)
    qb = q_ref[0].astype(BF16)

    def page_copies(slot, n):
        page = page_off + pt_ref[b, n_pages - 1 - n]
        return [pltpu.make_async_copy(cache_ref.at[page, :, pl.ds(r * SUBLANES, SUBLANES), :],
                                      buf_ref.at[slot, r], sem_ref.at[slot]) for r in range(tiles_per_key)]

    def attend(get_k, get_v, mask):
        z = jnp.concatenate([_dot_nt(qb[:, h * HEAD_DIM:(h + 1) * HEAD_DIM], get_k(h).astype(BF16))
                             for h in range(SB_HEADS)], axis=0) * scale
        carry = carry_ref[...]
        lb = jnp.minimum(z, 0.0) - jnp.log(1.0 + jnp.exp(-jnp.abs(z)))
        lk = lb - z
        if mask is not None:
            lb = jnp.where(mask, lb, 0.0)
            lk = jnp.where(mask, lk, 0.0)
        incl_mat = _suffix_sum_matrix(PAGE_SIZE)
        hi = lk.astype(BF16)
        lo = (lk - hi.astype(F32)).astype(BF16)
        incl = _dot(hi, incl_mat) + _dot(lo, incl_mat)
        a = jnp.exp(lb + (incl - lk + carry))
        if mask is not None:
            a = jnp.where(mask, a, 0.0)
        a = a.astype(BF16)
        for h in range(SB_HEADS):
            acc_ref[h * t:(h + 1) * t, :] += _dot(a[h * t:(h + 1) * t, :], get_v(h).astype(BF16))
        carry_ref[...] = carry + incl[:, 0:1]

    carry_ref[...] = jnp.zeros_like(carry_ref)
    acc_ref[...] = jnp.zeros_like(acc_ref)
    for cp in page_copies(0, 0):
        cp.start()
    new = _pad_rows(new_ref[0], PAGE_SIZE)
    attend(lambda h: new[:, h * HEAD_DIM:(h + 1) * HEAD_DIM],
           lambda h: new[:, SB_W + h * HEAD_DIM:SB_W + (h + 1) * HEAD_DIM], col < tq)

    def body(state):
        n, _ = state
        slot = n % 2
        for cp in page_copies(slot, n):
            cp.wait()

        @pl.when(n + 1 < n_pages)
        def _():
            for cp in page_copies(1 - slot, n + 1):
                cp.start()

        attend(lambda h: buf_ref[slot, h // SUBLANES, :, h % SUBLANES, :],
               lambda h: buf_ref[slot, (SB_HEADS + h) // SUBLANES, :, h % SUBLANES, :], None)
        live = jnp.max(carry_ref[...]) > EXP_UNDERFLOW
        return n + 1, live.astype(jnp.int32)

    n_done, _ = lax.while_loop(lambda st: (st[0] < n_pages) & (st[1] > 0), body, (jnp.int32(0), jnp.int32(1)))

    @pl.when(n_done < n_pages)
    def _():
        for cp in page_copies(n_done % 2, n_done):
            cp.wait()

    zg = z_ref[0]
    for h in range(SB_HEADS):
        sl = slice(h * HEAD_DIM, (h + 1) * HEAD_DIM)
        o_ref[0, :, sl] = (acc_ref[h * t:(h + 1) * t, :] * _silu(zg[:, sl])).astype(o_ref.dtype)


def sb_decode(q, new_kv, cache, page_off, page_table, z):
    b, t, _ = q.shape
    n_pages = page_table.shape[1]
    rows = SB_HEADS * t
    tiles_per_key = 2 * SB_HEADS // SUBLANES
    tok = lambda w: pl.BlockSpec((1, t, w), lambda bi, pt: (bi, 0, 0))
    grid_spec = pltpu.PrefetchScalarGridSpec(
        num_scalar_prefetch=1,
        grid=(b,),
        in_specs=[tok(SB_W), tok(2 * SB_W), tok(SB_W), pl.BlockSpec(memory_space=pl.ANY)],
        out_specs=tok(SB_W),
        scratch_shapes=[
            pltpu.VMEM((2, tiles_per_key, PAGE_SIZE, SUBLANES, HEAD_DIM), F32),
            pltpu.SemaphoreType.DMA((2,)),
            pltpu.VMEM((rows, 1), F32),
            pltpu.VMEM((rows, HEAD_DIM), F32),
        ],
    )
    return pl.pallas_call(
        functools.partial(_sb_decode_kernel, t=t, n_pages=n_pages, page_off=page_off),
        grid_spec=grid_spec,
        out_shape=jax.ShapeDtypeStruct((b, t, SB_W), BF16),
        compiler_params=_cparams(("arbitrary",)),
        name="sb_decode",
    )(page_table, q, new_kv, z, cache)


def _online_update(s, mask, m, l):
    sm = jnp.where(mask, s, NEG)
    m_new = jnp.maximum(m, jnp.max(sm, axis=1, keepdims=True))
    alpha = jnp.exp(m - m_new)
    p = jnp.where(mask, jnp.exp(sm - m_new), 0.0)
    l_new = alpha * l + jnp.sum(p, axis=1, keepdims=True)
    return p, alpha, m_new, l_new


def _flash_step(s, v, m_ref, l_ref, acc_ref):
    m_old = m_ref[...]
    m_new = jnp.maximum(m_old, jnp.max(s, axis=1, keepdims=True))
    alpha = jnp.exp(m_old - m_new)
    p = jnp.exp(s - m_new)
    l_ref[...] = alpha * l_ref[...] + jnp.sum(p, axis=1, keepdims=True)
    m_ref[...] = m_new
    acc_ref[...] = alpha * acc_ref[...] + _dot(p.astype(BF16), v)


def _flash_step_t(s_t, v_t, m_ref, l_ref, acc_ref):
    m_old = m_ref[...]
    m_new = jnp.maximum(m_old, jnp.max(s_t, axis=0, keepdims=True))
    alpha = jnp.exp(m_old - m_new)
    p = jnp.exp(s_t - m_new)
    l_ref[...] = alpha * l_ref[...] + jnp.sum(p, axis=0, keepdims=True)
    m_ref[...] = m_new
    acc_ref[...] = alpha * acc_ref[...] + _dot(v_t, p.astype(BF16))


def _pipelined_sweep(n, scores, fold):
    s = scores(0, True)

    def body(c, s):
        s_next = scores(c + 1, False)
        fold(s, c)
        return s_next

    s = lax.fori_loop(0, n - 1, body, s)

    @pl.when(n >= 1)
    def _():
        s_last = scores(n, True)
        fold(s, n - 1)
        fold(s_last, n)

    @pl.when(n == 0)
    def _():
        fold(s, 0)


def _normalise(acc, l):
    return acc / jnp.maximum(l, 1e-30)


def _diff_lambda(lq1_ref, lk1_ref, lq2_ref, lk2_ref, lambda_init):
    d1 = jnp.sum(lq1_ref[...] * lk1_ref[...], axis=1, keepdims=True)
    d2 = jnp.sum(lq2_ref[...] * lk2_ref[...], axis=1, keepdims=True)
    return jnp.exp(d1) - jnp.exp(d2) + lambda_init


def _diff_finish(o, hg, zg, lambda_init):
    ms = jnp.mean(o * o, axis=-1, keepdims=True)
    o = o * lax.rsqrt(ms + EPS) * hg * (1.0 - lambda_init)
    return o * _silu(zg)


def _diff_prompt_kernel(slopes_ref, q_ref, k_ref, v_ref, z_ref, lq1_ref, lk1_ref, lq2_ref, lk2_ref, hg_ref, o_ref,
                        vt_ref, m_ref, l_ref, acc_ref, *, tq, tk, lambda_init):
    h = pl.program_id(1)
    i = pl.program_id(2)
    slope = slopes_ref[h]
    t_total = v_ref.shape[1]

    @pl.when(i == 0)
    def _():
        for c in range(t_total // tk):
            vt_ref[c] = v_ref[0, c * tk:(c + 1) * tk, :].T.astype(BF16)

    q = q_ref[0] * DIFF_HALF ** -0.5
    qs = [q[:, :DIFF_HALF].astype(BF16), q[:, DIFF_HALF:].astype(BF16)]
    rc = _iota((tk, tq), 1) - _iota((tk, tq), 0)
    bias_rc = slope * rc.astype(F32)
    m_ref[...] = jnp.full_like(m_ref, NEG)
    l_ref[...] = jnp.zeros_like(l_ref)
    acc_ref[...] = jnp.zeros_like(acc_ref)

    def scores(j, edge):
        start = pl.multiple_of(j * tk, tk)
        k = k_ref[0, pl.ds(start, tk), :]
        off = (i - j) * tk
        bias = bias_rc + slope * off.astype(F32)
        out = []
        for c in range(2):
            s = _dot_nt(k[:, c * DIFF_HALF:(c + 1) * DIFF_HALF].astype(BF16), qs[c]) - bias
            out.append(jnp.where(rc + off >= 0, s, NEG) if edge else s)
        return tuple(out)

    def fold(s, j):
        vt = vt_ref[j]
        for c in range(2):
            _flash_step_t(s[c], vt, m_ref.at[c], l_ref.at[c], acc_ref.at[c])

    _pipelined_sweep(i, scores, fold)
    lam = _diff_lambda(lq1_ref, lk1_ref, lq2_ref, lk2_ref, lambda_init)
    o_t = _normalise(acc_ref[0], l_ref[0]) - lam * _normalise(acc_ref[1], l_ref[1])
    o_ref[0] = _diff_finish(o_t.T, hg_ref[...], z_ref[0], lambda_init).astype(o_ref.dtype)


def _smem_spec():
    return pl.BlockSpec(memory_space=pltpu.SMEM)


def diff_prompt(q, kv, z, lq1, lk1, lq2, lk2, head_g, lambda_init, *, tq=256, tk=256):
    b, t, _ = q.shape
    tq = min(tq, t)
    tk = min(tk, tq)
    assert tq == tk and t % tq == 0
    hq = pl.BlockSpec((1, tq, DIFF_VDIM), lambda bi, h, i: (bi, i, h))
    vec = lambda w: pl.BlockSpec((1, w), lambda bi, h, i: (0, 0))
    return pl.pallas_call(
        functools.partial(_diff_prompt_kernel, tq=tq, tk=tk, lambda_init=lambda_init),
        grid=(b, DIFF_HEADS, t // tq),
        in_specs=[
            _smem_spec(),
            hq,
            pl.BlockSpec((1, t, DIFF_VDIM), lambda bi, h, i: (bi, 0, h)),
            pl.BlockSpec((1, t, DIFF_VDIM), lambda bi, h, i: (bi, 0, DIFF_HEADS + h)),
            hq,
            vec(DIFF_HALF), vec(DIFF_HALF), vec(DIFF_HALF), vec(DIFF_HALF), vec(DIFF_VDIM),
        ],
        out_specs=hq,
        out_shape=jax.ShapeDtypeStruct((b, t, C_W), BF16),
        scratch_shapes=[pltpu.VMEM((t // tk, DIFF_VDIM, tk), BF16),
                        pltpu.VMEM((2, 1, tq), F32), pltpu.VMEM((2, 1, tq), F32),
                        pltpu.VMEM((2, DIFF_VDIM, tq), F32)],
        compiler_params=_cparams(("parallel", "parallel", "arbitrary")),
        name="diff_prompt",
    )(jnp.asarray(_alibi_slopes(DIFF_HEADS)), q, kv, kv, z,
      lq1.reshape(1, -1), lk1.reshape(1, -1), lq2.reshape(1, -1), lk2.reshape(1, -1), head_g.reshape(1, -1))


DIFF_PAGES_PER_STEP = 2


def _diff_decode_kernel(pt_ref, q_ref, knew_ref, vnew_ref, *refs, t, n_pages, pps, lambda_init):
    page_refs = refs[:2 * pps]
    (z_ref, slope_ref, lq1_ref, lk1_ref, lq2_ref, lk2_ref, hg_ref, o_ref, qs_ref, bias_ref, biasn_ref, m_ref,
     l_ref, acc_ref) = refs[2 * pps:]
    p = pl.program_id(1)
    n_steps = n_pages // pps
    scale = DIFF_HALF ** -0.5
    n_tiles = DIFF_HEADS // SUBLANES
    hr = SUBLANES * t
    past = n_pages * PAGE_SIZE

    def tile_bias(a, n_keys, causal):
        shape = (hr, n_keys * SUBLANES)
        r, c = _iota(shape, 0), _iota(shape, 1)
        ok = (r // t) == (c % SUBLANES)
        rel = r % t - c // SUBLANES
        if causal:
            ok = ok & (rel >= 0)
        return jnp.where(ok, -slope_ref[a] * rel.astype(F32), NEG)

    def attend(a, get_k, v, bias):
        s = jnp.concatenate([_dot_nt(qs_ref[c, a], get_k(c).astype(BF16)) + bias for c in range(2)], axis=0)
        m_old = m_ref[a]
        m_new = jnp.maximum(m_old, jnp.max(s, axis=1, keepdims=True))
        alpha = jnp.exp(m_old - m_new)
        pr = jnp.exp(s - m_new)
        l_ref[a] = alpha * l_ref[a] + jnp.sum(pr, axis=1, keepdims=True)
        m_ref[a] = m_new
        acc_ref[a] = alpha * acc_ref[a] + _dot(pr.astype(BF16), v.astype(BF16))

    @pl.when(p == 0)
    def _():
        q = q_ref[0] * scale
        for a in range(n_tiles):
            for c in range(2):
                cols = [(a * SUBLANES + hl) * DIFF_VDIM + c * DIFF_HALF for hl in range(SUBLANES)]
                qs_ref[c, a] = jnp.concatenate([q[:, o:o + DIFF_HALF] for o in cols], axis=0).astype(BF16)
            bias_ref[a] = tile_bias(a, PAGE_SIZE, False)
            biasn_ref[a] = tile_bias(a, t, True)
        m_ref[...] = jnp.full_like(m_ref, NEG)
        l_ref[...] = jnp.zeros_like(l_ref)
        acc_ref[...] = jnp.zeros_like(acc_ref)

    @pl.when(p < n_steps)
    def _():
        for k in range(pps):
            kpage_ref, vpage_ref = page_refs[2 * k], page_refs[2 * k + 1]
            base = (past - (p * pps + k) * PAGE_SIZE).astype(F32)
            for a in range(n_tiles):
                rows = slice(a * SUBLANES, (a + 1) * SUBLANES)
                get_k = lambda c: kpage_ref[:, rows, c * DIFF_HALF:(c + 1) * DIFF_HALF].reshape(
                    PAGE_SIZE * SUBLANES, DIFF_HALF)
                v = vpage_ref[:, rows, :].reshape(PAGE_SIZE * SUBLANES, DIFF_VDIM)
                attend(a, get_k, v, bias_ref[a] - slope_ref[a] * base)

    @pl.when(p == n_steps)
    def _():
        lam = _diff_lambda(lq1_ref, lk1_ref, lq2_ref, lk2_ref, lambda_init)
        zg = z_ref[0]
        hg = hg_ref[...]
        for a in range(n_tiles):
            rows = slice(a * SUBLANES, (a + 1) * SUBLANES)
            get_k = lambda c: knew_ref[0, :, rows, c * DIFF_HALF:(c + 1) * DIFF_HALF].reshape(t * SUBLANES, DIFF_HALF)
            attend(a, get_k, vnew_ref[0, :, rows, :].reshape(t * SUBLANES, DIFF_VDIM), biasn_ref[a])
            o = _normalise(acc_ref[a], l_ref[a])
            for hl in range(SUBLANES):
                r1 = slice(hl * t, (hl + 1) * t)
                r2 = slice(hr + hl * t, hr + (hl + 1) * t)
                sl = slice((a * SUBLANES + hl) * DIFF_VDIM, (a * SUBLANES + hl + 1) * DIFF_VDIM)
                o_ref[0, :, sl] = _diff_finish(o[r1] - lam * o[r2], hg, zg[:, sl], lambda_init).astype(o_ref.dtype)


def diff_decode(q, new_kv, cache, page_off, page_table, z, lq1, lk1, lq2, lk2, head_g, lambda_init):
    b, t, _ = q.shape
    n_pages = page_table.shape[1]
    n_tiles = DIFF_HEADS // SUBLANES
    hr = SUBLANES * t
    slopes = np.repeat(_alibi_slopes(DIFF_HEADS), t).reshape(n_tiles, hr, 1)
    tok = lambda w: pl.BlockSpec((1, t, w), lambda bi, p, pt: (bi, 0, 0))
    vec = lambda w: pl.BlockSpec((1, w), lambda bi, p, pt: (0, 0))
    new_spec = lambda kv: pl.BlockSpec((1, t, DIFF_HEADS, DIFF_VDIM), lambda bi, p, pt: (bi, 0, kv, 0))
    pps = math.gcd(n_pages, DIFF_PAGES_PER_STEP)
    n_steps = n_pages // pps
    page_spec = lambda k, kv: pl.BlockSpec(
        (None, PAGE_SIZE, DIFF_HEADS, DIFF_VDIM),
        lambda bi, p, pt: (page_off + pt[bi, jnp.minimum(p, n_steps - 1) * pps + k], 0, kv, 0))
    grid_spec = pltpu.PrefetchScalarGridSpec(
        num_scalar_prefetch=1,
        grid=(b, n_steps + 1),
        in_specs=[
            tok(C_W), new_spec(0), new_spec(1),
            *[page_spec(k, kv) for k in range(pps) for kv in range(2)],
            tok(C_W),
            pl.BlockSpec((n_tiles, hr, 1), lambda bi, p, pt: (0, 0, 0)),
            vec(DIFF_HALF), vec(DIFF_HALF), vec(DIFF_HALF), vec(DIFF_HALF), vec(DIFF_VDIM),
        ],
        out_specs=tok(C_W),
        scratch_shapes=[
            pltpu.VMEM((2, n_tiles, hr, DIFF_HALF), BF16),
            pltpu.VMEM((n_tiles, hr, PAGE_SIZE * SUBLANES), F32),
            pltpu.VMEM((n_tiles, hr, t * SUBLANES), F32),
            pltpu.VMEM((n_tiles, 2 * hr, 1), F32),
            pltpu.VMEM((n_tiles, 2 * hr, 1), F32),
            pltpu.VMEM((n_tiles, 2 * hr, DIFF_VDIM), F32),
        ],
    )
    return pl.pallas_call(
        functools.partial(_diff_decode_kernel, t=t, n_pages=n_pages, pps=pps, lambda_init=lambda_init),
        grid_spec=grid_spec,
        out_shape=jax.ShapeDtypeStruct((b, t, C_W), BF16),
        compiler_params=_cparams(("parallel", "arbitrary")),
        name="diff_decode",
    )(page_table, q, new_kv, new_kv, *([cache] * (2 * pps)), z, jnp.asarray(slopes),
      lq1.reshape(1, -1), lk1.reshape(1, -1), lq2.reshape(1, -1), lk2.reshape(1, -1), head_g.reshape(1, -1))


def _masked_softmax_parts(parts):
    m = None
    for s, mask in parts:
        pm = jnp.max(jnp.where(mask, s, NEG), axis=1, keepdims=True)
        m = pm if m is None else jnp.maximum(m, pm)
    es = [jnp.where(mask, jnp.exp(jnp.where(mask, s, NEG) - m), 0.0) for s, mask in parts]
    den = None
    for e in es:
        d = jnp.sum(e, axis=1, keepdims=True)
        den = d if den is None else den + d
    den = jnp.maximum(den, 1e-30)
    return [e / den for e in es]


def _nsa_prompt_kernel(slopes_ref, q_ref, kc_ref, vc_ref, ks_ref, vs_ref, kw_ref, vw_ref, lg_ref, z_ref,
                       wk_ref, wv_ref, expand_ref, o_ref, kcmp_ref, vcmp_ref, vst_ref, vwt_ref, selbias_ref, m_ref,
                       l_ref, acc_ref, *, tq, t_total):
    g = pl.program_id(1)
    i = pl.program_id(2)
    scale = HEAD_DIM ** -0.5
    nb = t_total // SEL_BLOCK
    rows = NSA_GROUP * tq
    tk = tq

    @pl.when(i == 0)
    def _():
        for src, w_ref, dst in ((kc_ref, wk_ref, kcmp_ref), (vc_ref, wv_ref, vcmp_ref)):
            x = src[0].reshape(nb, SEL_BLOCK, HEAD_DIM)
            w = w_ref[0][None]
            even = jnp.sum(x[:, :CMP_BLOCK, :] * w, axis=1)
            odd = jnp.sum(x[:, CMP_BLOCK:, :] * w, axis=1)
            pad = jnp.zeros((LANES - 2 * nb, HEAD_DIM), F32)
            dst[...] = jnp.concatenate([even, odd, pad], axis=0).astype(BF16)
        for src, dst in ((vs_ref, vst_ref), (vw_ref, vwt_ref)):
            for c in range(t_total // tk):
                dst[c] = src[0, c * tk:(c + 1) * tk, :].T.astype(BF16)

    q = q_ref[0] * scale
    q4 = jnp.concatenate([q[:, zz * HEAD_DIM:(zz + 1) * HEAD_DIM] for zz in range(NSA_GROUP)], axis=0).astype(BF16)
    qpos1 = i * tq + _iota((tq, 1), 0)
    qpos4 = jnp.concatenate([qpos1] * NSA_GROUP, axis=0)
    slope4 = jnp.concatenate(
        [jnp.full((tq, 1), slopes_ref[g * NSA_GROUP + zz], F32) for zz in range(NSA_GROUP)], axis=0)
    col = _iota((1, LANES), 1)

    cidx = jnp.where(col < nb, 2 * col, 2 * (col - nb) + 1)
    dist_c = qpos4 - (cidx * CMP_BLOCK + (CMP_BLOCK - 1))
    s_c = _dot_nt(q4, kcmp_ref[...]) - slope4 * dist_c.astype(F32)
    (p_c,) = _masked_softmax_parts([(s_c, (dist_c >= 0) & (col < 2 * nb))])
    o_cmp = _dot(p_c.astype(BF16), vcmp_ref[...])
    imp = p_c[0:tq]
    for zz in range(1, NSA_GROUP):
        imp = imp + p_c[zz * tq:(zz + 1) * tq]
    pair = imp + pltpu.roll(imp, LANES - nb, 1)

    cur = qpos1 // SEL_BLOCK
    valid = col * SEL_BLOCK <= qpos1
    forced = (col == cur) | (col == 0)
    score = jnp.where(forced, FORCED_SCORE, jnp.where(valid, pair, -1.0))
    score = jnp.where(col < nb, score, -2.0)
    score_t = score.T
    cand = score_t[:nb]
    blk = _iota((nb, 1), 0)
    rank = jnp.zeros((nb, tq), F32)
    for j in range(nb):
        r = score_t[j:j + 1, :]
        ge = jnp.where(r >= cand, 1.0, 0.0)
        gt = jnp.where(r > cand, 1.0, 0.0)
        rank = rank + jnp.where(blk > j, ge, gt)
    sel_t = (rank < float(min(SEL_TOPK, nb))).astype(F32)
    sel_rows = jnp.concatenate([sel_t, jnp.zeros((LANES - nb, tq), F32)], axis=0).astype(BF16)
    selbias_ref[...] = (_dot(expand_ref[...], sel_rows) - 1.0) * (-NEG)

    tile = lambda x: jnp.concatenate([x] * NSA_GROUP, axis=1)
    rc4 = tile(_iota((tk, tq), 1) - _iota((tk, tq), 0))
    slope_row = jnp.concatenate(
        [jnp.full((1, tq), slopes_ref[g * NSA_GROUP + zz], F32) for zz in range(NSA_GROUP)], axis=1)
    bias_rc = slope_row * rc4.astype(F32)

    def reset():
        m_ref[...] = jnp.full_like(m_ref, NEG)
        l_ref[...] = jnp.zeros_like(l_ref)
        acc_ref[...] = jnp.zeros_like(acc_ref)

    def scores(k_ref, j, admit):
        start = pl.multiple_of(j * tk, tk)
        k = k_ref[0, pl.ds(start, tk), :].astype(BF16)
        off = (i - j) * tk
        s = _dot_nt(k, q4) - (bias_rc + slope_row * off.astype(F32))
        return admit(s, start, rc4 + off)

    def slc_admit(edge):
        def admit(s, start, dist):
            s = s + tile(selbias_ref[pl.ds(start, tk), :])
            return jnp.where(dist >= 0, s, NEG) if edge else s
        return admit

    reset()
    _pipelined_sweep(i, lambda c, edge: scores(ks_ref, c, slc_admit(edge)),
                     lambda s, c: _flash_step_t(s, vst_ref[c], m_ref, l_ref, acc_ref))
    o_slc_t = _normalise(acc_ref[...], l_ref[...])

    def win_admit(edge):
        return (lambda s, start, dist: jnp.where((dist >= 0) & (dist < WINDOW), s, NEG)) if edge else (
            lambda s, start, dist: s)

    reset()
    _pipelined_sweep(jnp.minimum(i, WINDOW // tk), lambda c, edge: scores(kw_ref, i - c, win_admit(edge)),
                     lambda s, c: _flash_step_t(s, vwt_ref[i - c], m_ref, l_ref, acc_ref))
    o_win_t = _normalise(acc_ref[...], l_ref[...])

    gates = pltpu.roll(_sigmoid(lg_ref[0]), (LANES - 3 * NSA_GROUP * g) % LANES, 1)
    zg = z_ref[0]
    for zz in range(NSA_GROUP):
        r = slice(zz * tq, (zz + 1) * tq)
        mix = (gates[:, 3 * zz:3 * zz + 1] * o_cmp[r] + gates[:, 3 * zz + 1:3 * zz + 2] * o_slc_t[:, r].T
               + gates[:, 3 * zz + 2:3 * zz + 3] * o_win_t[:, r].T)
        sl = slice(zz * HEAD_DIM, (zz + 1) * HEAD_DIM)
        o_ref[0, :, sl] = (mix * _silu(zg[:, sl])).astype(o_ref.dtype)


def _cmp_weight_rows(w):
    return jnp.broadcast_to(w.T[:, :, None], (NSA_KV_HEADS, CMP_BLOCK, HEAD_DIM)).astype(F32)


def nsa_prompt(q, nskv, win, logits, z, cmp_wk, cmp_wv, *, tq=128):
    b, t, _ = q.shape
    nb = t // SEL_BLOCK
    assert t % tq == 0 and 2 * nb <= LANES and WINDOW % tq == 0
    gw = NSA_GROUP * HEAD_DIM
    expand = ((np.arange(t)[:, None] // SEL_BLOCK) == np.arange(LANES)[None, :]).astype(np.float32)
    kvspec = lambda kind: pl.BlockSpec((1, t, HEAD_DIM), lambda bi, g, i: (bi, 0, kind * NSA_KV_HEADS + g))
    qspec = pl.BlockSpec((1, tq, gw), lambda bi, g, i: (bi, i, g))
    wspec = pl.BlockSpec((1, CMP_BLOCK, HEAD_DIM), lambda bi, g, i: (g, 0, 0))
    return pl.pallas_call(
        functools.partial(_nsa_prompt_kernel, tq=tq, t_total=t),
        grid=(b, NSA_KV_HEADS, t // tq),
        in_specs=[
            _smem_spec(),
            qspec,
            kvspec(0), kvspec(1), kvspec(2), kvspec(3),
            kvspec(0), kvspec(1),
            pl.BlockSpec((1, tq, LANES), lambda bi, g, i: (bi, i, 0)),
            qspec,
            wspec, wspec,
            pl.BlockSpec((t, LANES), lambda bi, g, i: (0, 0)),
        ],
        out_specs=qspec,
        out_shape=jax.ShapeDtypeStruct((b, t, NSA_W), BF16),
        scratch_shapes=[
            pltpu.VMEM((LANES, HEAD_DIM), BF16),
            pltpu.VMEM((LANES, HEAD_DIM), BF16),
            pltpu.VMEM((t // tq, HEAD_DIM, tq), BF16),
            pltpu.VMEM((t // tq, HEAD_DIM, tq), BF16),
            pltpu.VMEM((t, tq), F32),
            pltpu.VMEM((1, NSA_GROUP * tq), F32),
            pltpu.VMEM((1, NSA_GROUP * tq), F32),
            pltpu.VMEM((HEAD_DIM, NSA_GROUP * tq), F32),
        ],
        compiler_params=_cparams(("parallel", "parallel", "arbitrary")),
        name="nsa_prompt",
    )(jnp.asarray(_alibi_slopes(NSA_HEADS)), q, nskv, nskv, nskv, nskv, win, win, logits, z,
      _cmp_weight_rows(cmp_wk), _cmp_weight_rows(cmp_wv), jnp.asarray(expand, BF16))


NSA_PAGES_PER_STEP = 4


def _nsa_compress_kernel(pt_ref, *refs):
    page_refs, w_ref, even_ref, odd_ref = refs[:-3], refs[-3], refs[-2], refs[-1]
    n_cmp = PAGE_SIZE // CMP_BLOCK
    w = w_ref[...][None]
    evens, odds = [], []
    for page_ref in page_refs:
        x = page_ref[...].reshape(n_cmp, CMP_BLOCK, 2 * NSA_KV_HEADS, HEAD_DIM)
        c = jnp.sum(x * w, axis=1)
        evens += [c[r:r + 1] for r in range(0, n_cmp, 2)]
        odds += [c[r:r + 1] for r in range(1, n_cmp, 2)]
    even_ref[0] = jnp.concatenate(evens, axis=0)
    odd_ref[0] = jnp.concatenate(odds, axis=0)


def nsa_compress_pages(cache, page_off, page_table, cmp_wk, cmp_wv):
    b, n_pages = page_table.shape
    pps = math.gcd(n_pages, NSA_PAGES_PER_STEP)
    half_blocks = PAGE_SIZE // CMP_BLOCK // 2
    w = jnp.concatenate([cmp_wk, cmp_wv], axis=1).astype(F32)
    w = jnp.broadcast_to(w[:, :, None], (CMP_BLOCK, 2 * NSA_KV_HEADS, HEAD_DIM))
    cmp_spec = pl.BlockSpec((1, pps * half_blocks, 2 * NSA_KV_HEADS, HEAD_DIM), lambda bi, p, pt: (bi, p, 0, 0))
    page_spec = lambda k: pl.BlockSpec((None, PAGE_SIZE, 2 * NSA_KV_HEADS, HEAD_DIM),
                                       lambda bi, p, pt: (page_off + pt[bi, p * pps + k], 0, 0, 0))
    grid_spec = pltpu.PrefetchScalarGridSpec(
        num_scalar_prefetch=1,
        grid=(b, n_pages // pps),
        in_specs=[page_spec(k) for k in range(pps)]
        + [pl.BlockSpec((CMP_BLOCK, 2 * NSA_KV_HEADS, HEAD_DIM), lambda bi, p, pt: (0, 0, 0))],
        out_specs=[cmp_spec, cmp_spec],
    )
    return pl.pallas_call(
        _nsa_compress_kernel,
        grid_spec=grid_spec,
        out_shape=[jax.ShapeDtypeStruct((b, n_pages * half_blocks, 2 * NSA_KV_HEADS, HEAD_DIM), F32)] * 2,
        compiler_params=_cparams(("parallel", "parallel")),
        name="nsa_compress_pages",
    )(page_table, *([cache] * pps), w)


def _nsa_decode_kernel(pt_ref, q_ref, even_ref, odd_ref, *refs, t, n_pages, pps):
    page_refs = refs[:pps]
    (new_ref, wst_ref, wnew_ref, lg_ref, z_ref, slope_ref, o_ref, qg_ref, sel_ref, ocmp_ref, m_ref, l_ref,
     acc_ref) = refs[pps:]
    p = pl.program_id(1)
    n_steps = n_pages // pps
    scale = HEAD_DIM ** -0.5
    rows = NSA_HEADS * t
    grp_rows = NSA_GROUP * t
    past = n_pages * PAGE_SIZE
    n_blk = past // SEL_BLOCK
    half = n_blk
    tq = _iota((rows, 1), 0) % t
    qpos = past + tq
    slope = slope_ref[...]
    col = _iota((1, PAGE_SIZE), 1)

    def scores(get_k):
        return jnp.concatenate([_dot_nt(qg_ref[gg], get_k(gg).astype(BF16)) for gg in range(NSA_KV_HEADS)],
                               axis=0) * scale

    def group_pv(pr, get_v):
        return jnp.concatenate([_dot(pr[gg * grp_rows:(gg + 1) * grp_rows], get_v(gg).astype(BF16))
                                for gg in range(NSA_KV_HEADS)], axis=0)

    def rows_kv(ref3):
        return (lambda gg: ref3[:, gg, :]), (lambda gg: ref3[:, NSA_KV_HEADS + gg, :])

    def lanes_kv(x, first_block):
        blk = lambda i: x[:, (first_block + i) * HEAD_DIM:(first_block + i + 1) * HEAD_DIM]
        return (lambda gg: blk(gg)), (lambda gg: blk(NSA_KV_HEADS + gg))

    @pl.when(p == 0)
    def _():
        q = q_ref[0]
        for gg in range(NSA_KV_HEADS):
            heads = range(gg * NSA_GROUP, (gg + 1) * NSA_GROUP)
            qg_ref[gg] = jnp.concatenate([q[:, h * HEAD_DIM:(h + 1) * HEAD_DIM] for h in heads], axis=0).astype(BF16)
        parts, vals = [], []
        for par, cmp_ref in enumerate((even_ref, odd_ref)):
            get_k, get_v = rows_kv(cmp_ref.at[0])
            vals.append(get_v)
            c_end = (2 * _iota((1, half), 1) + par) * CMP_BLOCK + (CMP_BLOCK - 1)
            dist = qpos - c_end
            parts.append((scores(get_k) - slope * dist.astype(F32), dist >= 0))
        p_e, p_o = _masked_softmax_parts(parts)
        ocmp_ref[...] = group_pv(p_e.astype(BF16), vals[0]) + group_pv(p_o.astype(BF16), vals[1])
        pe = p_e + p_o
        blkcol = _iota((1, n_blk), 1)
        picked = []
        for gg in range(NSA_KV_HEADS):
            imp = pe[gg * grp_rows:gg * grp_rows + t]
            for zz in range(1, NSA_GROUP):
                imp = imp + pe[gg * grp_rows + zz * t:gg * grp_rows + (zz + 1) * t]
            sc = jnp.where(blkcol == 0, -1.0, imp)
            sel = (blkcol == 0)
            for _ in range(min(SEL_TOPK, n_blk + 1) - 2):
                mx = jnp.max(sc, axis=1, keepdims=True)
                first = jnp.min(jnp.where(sc == mx, blkcol, n_blk), axis=1, keepdims=True)
                hit = blkcol == first
                sel = sel | hit
                sc = jnp.where(hit, -1.0, sc)
            self32 = sel.astype(F32)
            picked.extend([self32] * NSA_GROUP)
        sel_ref[...] = jnp.concatenate(picked, axis=0).astype(BF16)
        m_ref[...] = jnp.full_like(m_ref, NEG)
        l_ref[...] = jnp.zeros_like(l_ref)
        acc_ref[...] = jnp.zeros_like(acc_ref)

    def attend(get_k, get_v, dist, mask):
        s = scores(get_k) - slope * dist.astype(F32)
        pr, alpha, m_new, l_new = _online_update(s, mask, m_ref[...], l_ref[...])
        m_ref[...] = m_new
        l_ref[...] = l_new
        acc_ref[...] = alpha * acc_ref[...] + group_pv(pr.astype(BF16), get_v)

    @pl.when(p < n_steps)
    def _():
        keys = pps * PAGE_SIZE
        blocks_per_step = keys // SEL_BLOCK
        expand = (_iota((n_blk, keys), 0) == blocks_per_step * p + _iota((n_blk, keys), 1) // SEL_BLOCK)
        chosen = _dot(sel_ref[...], expand.astype(BF16)) > 0.5
        dist = qpos - (p * keys + _iota((1, keys), 1))
        getters = [rows_kv(page_ref) for page_ref in page_refs]
        get_k = lambda gg: jnp.concatenate([gk(gg) for gk, _ in getters], axis=0)
        get_v = lambda gg: jnp.concatenate([gv(gg) for _, gv in getters], axis=0)
        attend(get_k, get_v, dist, chosen & (dist >= 0))

    @pl.when(p == n_steps)
    def _():
        new = _pad_rows(new_ref[0], PAGE_SIZE)
        dist = qpos - (past + col)
        get_k, get_v = lanes_kv(new, 2 * NSA_KV_HEADS)
        attend(get_k, get_v, dist, (dist >= 0) & (col < t))
        o_slc = _normalise(acc_ref[...], l_ref[...])
        n_state = wst_ref.shape[1]
        dist_s = qpos - (past - n_state + _iota((1, n_state), 1))
        ks_state, vs_state = rows_kv(wst_ref.at[0])
        ks_new, vs_new = lanes_kv(_pad_rows(wnew_ref[0], PAGE_SIZE), 0)
        p_s, p_n = _masked_softmax_parts([
            (scores(ks_state) - slope * dist_s.astype(F32), (dist_s >= 0) & (dist_s < WINDOW)),
            (scores(ks_new) - slope * dist.astype(F32), (dist >= 0) & (dist < WINDOW) & (col < t)),
        ])
        o_win = group_pv(p_s.astype(BF16), vs_state) + group_pv(p_n.astype(BF16), vs_new)
        o_cmp = ocmp_ref[...]
        gates = _sigmoid(lg_ref[0])
        zg = z_ref[0]
        for h in range(NSA_HEADS):
            r = slice(h * t, (h + 1) * t)
            mix = (gates[:, 3 * h:3 * h + 1] * o_cmp[r] + gates[:, 3 * h + 1:3 * h + 2] * o_slc[r]
                   + gates[:, 3 * h + 2:3 * h + 3] * o_win[r])
            sl = slice(h * HEAD_DIM, (h + 1) * HEAD_DIM)
            o_ref[0, :, sl] = (mix * _silu(zg[:, sl])).astype(o_ref.dtype)


def nsa_decode(q, cmp_even, cmp_odd, cache, page_off, page_table, new_nskv, win_state, win_new, logits, z):
    b, t, _ = q.shape
    n_pages = page_table.shape[1]
    past = n_pages * PAGE_SIZE
    n_blk = past // SEL_BLOCK
    n_state = win_state.shape[1]
    assert past % SEL_BLOCK == 0 and t < CMP_BLOCK and n_blk + 1 >= SEL_TOPK and n_state >= WINDOW - 1
    rows = NSA_HEADS * t
    slopes = np.repeat(_alibi_slopes(NSA_HEADS), t).reshape(rows, 1)
    pps = math.gcd(n_pages, NSA_PAGES_PER_STEP)
    n_steps = n_pages // pps
    tok = lambda w: pl.BlockSpec((1, t, w), lambda bi, p, pt: (bi, 0, 0))
    page_spec = lambda k: pl.BlockSpec(
        (None, PAGE_SIZE, 2 * NSA_KV_HEADS, HEAD_DIM),
        lambda bi, p, pt: (page_off + pt[bi, jnp.minimum(p, n_steps - 1) * pps + k], 0, 1, 0))
    grid_spec = pltpu.PrefetchScalarGridSpec(
        num_scalar_prefetch=1,
        grid=(b, n_steps + 1),
        in_specs=[
            tok(NSA_W),
            pl.BlockSpec((1, n_blk, 2 * NSA_KV_HEADS, HEAD_DIM), lambda bi, p, pt: (bi, 0, 0, 0)),
            pl.BlockSpec((1, n_blk, 2 * NSA_KV_HEADS, HEAD_DIM), lambda bi, p, pt: (bi, 0, 0, 0)),
            *[page_spec(k) for k in range(pps)],
            tok(4 * NSA_KV_W),
            pl.BlockSpec((1, n_state, 2 * NSA_KV_HEADS, HEAD_DIM), lambda bi, p, pt: (bi, 0, 0, 0)),
            tok(2 * NSA_KV_W),
            tok(LANES),
            tok(NSA_W),
            pl.BlockSpec((rows, 1), lambda bi, p, pt: (0, 0)),
        ],
        out_specs=tok(NSA_W),
        scratch_shapes=[
            pltpu.VMEM((NSA_KV_HEADS, NSA_GROUP * t, HEAD_DIM), BF16),
            pltpu.VMEM((rows, n_blk), BF16),
            pltpu.VMEM((rows, HEAD_DIM), F32),
            pltpu.VMEM((rows, 1), F32),
            pltpu.VMEM((rows, 1), F32),
            pltpu.VMEM((rows, HEAD_DIM), F32),
        ],
    )
    return pl.pallas_call(
        functools.partial(_nsa_decode_kernel, t=t, n_pages=n_pages, pps=pps),
        grid_spec=grid_spec,
        out_shape=jax.ShapeDtypeStruct((b, t, NSA_W), BF16),
        compiler_params=_cparams(("parallel", "arbitrary")),
        name="nsa_decode",
    )(page_table, q, cmp_even, cmp_odd, *([cache] * pps), new_nskv, win_state, win_new, logits, z, jnp.asarray(slopes))


def _ab_projections(hn, w_in):
    o = np.cumsum([0, SB_W, 2 * SB_W, SB_W, NSA_W, 4 * NSA_KV_W, 2 * NSA_KV_W, 3 * NSA_HEADS, NSA_W])
    names = ("sbq", "sbkv", "sbz", "nsq", "nskv", "win")
    proj = {name: matmul([hn], w_in, col0=int(o[c]), n=int(o[c + 1] - o[c])) for c, name in enumerate(names)}
    proj["logits"] = matmul([hn], jnp.pad(w_in[:, o[6]:o[7]], ((0, 0), (0, LANES - 3 * NSA_HEADS))))
    proj["nsz"] = matmul([hn], w_in[:, o[7]:o[8]])
    return proj


def _c_projections(hn, w_in):
    return dict(q=matmul([hn], w_in, col0=0, n=C_W), kv=matmul([hn], w_in, col0=C_W, n=2 * C_W),
                z=matmul([hn], w_in, col0=3 * C_W, n=C_W))


def kernel(x_prompt, x_sample, p_prompt, p_sample, cache_sb_kv, cache_nsa_kv, state_nsa_win_kv, cache_diff_kv,
           page_table, norm_g, w_in_ab, nsa_cmp_wk, nsa_cmp_wv, w_out_ab, w_in_c, diff_lq1, diff_lk1, diff_lq2,
           diff_lk2, diff_head_g, w_out_c, ple_norm_g, w_ple_gate, w_ple_proj, final_norm_g):
    bp, tp, d = x_prompt.shape
    bs, ts, _ = x_sample.shape
    depth = norm_g.shape[0]
    n_phys = cache_sb_kv.shape[1]

    def run(x, p_emb, sample):
        b, t, _ = x.shape
        h = x.reshape(b * t, d)
        sb_rows, nsa_rows, win_rows, diff_rows = [], [], [], []
        for i in range(depth):
            j = i // 2
            hn = rmsnorm(h, norm_g[i], BF16)
            if i % 2 == 0:
                proj = {name: y.reshape(b, t, -1) for name, y in _ab_projections(hn, w_in_ab[j]).items()}
                win_new = proj["win"].reshape(b, t, 2, NSA_KV_HEADS, HEAD_DIM)
                if sample:
                    sb_mixed = sb_decode(proj["sbq"], proj["sbkv"],
                                         cache_sb_kv.reshape(-1, PAGE_SIZE, 2 * SB_HEADS, HEAD_DIM), j * n_phys,
                                         page_table, proj["sbz"])
                    nsa_cache = cache_nsa_kv.reshape(-1, PAGE_SIZE, 4 * NSA_KV_HEADS, HEAD_DIM)
                    cmp_even, cmp_odd = nsa_compress_pages(nsa_cache, j * n_phys, page_table, nsa_cmp_wk[j],
                                                           nsa_cmp_wv[j])
                    win_rows_in = state_nsa_win_kv[j].reshape(b, -1, 2 * NSA_KV_HEADS, HEAD_DIM)
                    ns_mixed = nsa_decode(proj["nsq"], cmp_even, cmp_odd, nsa_cache, j * n_phys, page_table,
                                          proj["nskv"], win_rows_in, proj["win"], proj["logits"], proj["nsz"])
                    win_all = jnp.concatenate([state_nsa_win_kv[j], win_new], axis=1)
                else:
                    sb_mixed = sb_prompt(proj["sbq"], proj["sbkv"], proj["sbz"])
                    ns_mixed = nsa_prompt(proj["nsq"], proj["nskv"], proj["win"], proj["logits"], proj["nsz"],
                                          nsa_cmp_wk[j], nsa_cmp_wv[j])
                    win_all = win_new
                h = matmul([sb_mixed.reshape(b * t, -1), ns_mixed.reshape(b * t, -1)], w_out_ab[j], residual=h)
                sb_rows.append(proj["sbkv"].reshape(b, t, 2, SB_HEADS, HEAD_DIM))
                nsa_rows.append(proj["nskv"].reshape(b, t, 4, NSA_KV_HEADS, HEAD_DIM))
                keep = min(WINDOW, win_all.shape[1])
                win_rows.append(win_all[:, win_all.shape[1] - keep:])
            else:
                lambda_init = 0.8 - 0.6 * math.exp(-0.3 * i)
                proj = {name: y.reshape(b, t, -1) for name, y in _c_projections(hn, w_in_c[j]).items()}
                lam_args = (diff_lq1[j], diff_lk1[j], diff_lq2[j], diff_lk2[j], diff_head_g[j], lambda_init)
                if sample:
                    mixed = diff_decode(proj["q"], proj["kv"].reshape(b, t, 2 * DIFF_HEADS, DIFF_VDIM),
                                        cache_diff_kv.reshape(-1, PAGE_SIZE, 2 * DIFF_HEADS, DIFF_VDIM), j * n_phys,
                                        page_table, proj["z"], *lam_args)
                else:
                    mixed = diff_prompt(proj["q"], proj["kv"], proj["z"], *lam_args)
                h = matmul([mixed.reshape(b * t, -1)], w_out_c[j], residual=h)
                diff_rows.append(proj["kv"].reshape(b, t, 2, DIFF_HEADS, DIFF_VDIM))
            hn2 = rmsnorm(h, ple_norm_g[i], BF16)
            h = ple(hn2, w_ple_gate, p_emb[i].reshape(b * t, -1).astype(BF16), w_ple_proj, h, i)
        y = rmsnorm(h, final_norm_g, F32).reshape(b, t, d)
        return y, jnp.stack(sb_rows), jnp.stack(nsa_rows), jnp.stack(win_rows), jnp.stack(diff_rows)

    y_p, sb_p, nsa_p, win_p, diff_p = run(x_prompt, p_prompt, False)
    y_s, sb_s, nsa_s, win_s, diff_s = run(x_sample, p_sample, True)
    return (y_p, y_s, sb_p, sb_s, nsa_p, nsa_s, win_p, win_s, diff_p, diff_s)
```

```python
import functools
import math

import jax
import jax.numpy as jnp
import numpy as np
from jax import lax
from jax.experimental import pallas as pl
from jax.experimental.pallas import tpu as pltpu

F32 = jnp.float32
BF16 = jnp.bfloat16

HEAD_DIM = 128
SB_HEADS = 16
NSA_HEADS = 16
NSA_KV_HEADS = 4
NSA_GROUP = NSA_HEADS // NSA_KV_HEADS
CMP_BLOCK = 32
SEL_BLOCK = 64
SEL_TOPK = 16
WINDOW = 512
DIFF_HEADS = 16
DIFF_HALF = 128
DIFF_VDIM = 2 * DIFF_HALF
PAGE_SIZE = 128
EPS = 1e-6
NEG = -1e30
FORCED_SCORE = 1e4

SB_W = SB_HEADS * HEAD_DIM
NSA_W = NSA_HEADS * HEAD_DIM
NSA_KV_W = NSA_KV_HEADS * HEAD_DIM
C_W = DIFF_HEADS * DIFF_VDIM
LANES = 128
SUBLANES = 8
EXP_UNDERFLOW = -110.0
VMEM_LIMIT = 56 * 1024 * 1024


def _alibi_slopes(n):
    return np.asarray(2.0 ** (-8.0 * np.arange(1, n + 1) / n), dtype=np.float32)


def _cparams(sem):
    return pltpu.CompilerParams(dimension_semantics=sem, vmem_limit_bytes=VMEM_LIMIT)


def _dot(a, b):
    return jnp.dot(a, b, preferred_element_type=F32)


def _dot_nt(a, b):
    return lax.dot_general(a, b, (((1,), (1,)), ((), ())), preferred_element_type=F32)


def _silu(z):
    return z * (1.0 / (1.0 + jnp.exp(-z)))


def _sigmoid(z):
    return 1.0 / (1.0 + jnp.exp(-z))


def _iota(shape, dim):
    return lax.broadcasted_iota(jnp.int32, shape, dim)


def _rmsnorm_kernel(x_ref, g_ref, o_ref):
    x = x_ref[...]
    ms = jnp.mean(x * x, axis=-1, keepdims=True)
    o_ref[...] = (x * lax.rsqrt(ms + EPS) * g_ref[...]).astype(o_ref.dtype)


def rmsnorm(x, g, out_dtype):
    m, d = x.shape
    tm = min(m, 256)
    return pl.pallas_call(
        _rmsnorm_kernel,
        grid=(m // tm,),
        in_specs=[pl.BlockSpec((tm, d), lambda i: (i, 0)), pl.BlockSpec((1, d), lambda i: (0, 0))],
        out_specs=pl.BlockSpec((tm, d), lambda i: (i, 0)),
        out_shape=jax.ShapeDtypeStruct((m, d), out_dtype),
        compiler_params=_cparams(("parallel",)),
        name="rmsnorm",
    )(x, g.reshape(1, d))


def _mm_kernel(*refs, n_parts, has_residual):
    a_refs, w_refs = refs[:n_parts], refs[n_parts:2 * n_parts]
    r_ref = refs[2 * n_parts] if has_residual else None
    o_ref = refs[2 * n_parts + has_residual]
    wb_refs = refs[2 * n_parts + has_residual + 1:]

    @pl.when(pl.program_id(1) == 0)
    def _():
        for w_ref, wb_ref in zip(w_refs, wb_refs):
            wb_ref[...] = w_ref[...].astype(BF16)

    acc = _dot(a_refs[0][...], wb_refs[0][...])
    for a_ref, wb_ref in zip(a_refs[1:], wb_refs[1:]):
        acc = acc + _dot(a_ref[...], wb_ref[...])
    if has_residual:
        acc = r_ref[...] + acc
    o_ref[...] = acc


def matmul(a_parts, w, *, col0=0, n=None, residual=None):
    m, kp = a_parts[0].shape
    n_parts = len(a_parts)
    assert all(a.shape == (m, kp) for a in a_parts) and w.shape[0] == n_parts * kp
    n = w.shape[1] - col0 if n is None else n
    tm = min(m, 1024)
    tn = min(n, 512)
    assert m % tm == 0 and n % tn == 0 and col0 % tn == 0
    in_specs = [pl.BlockSpec((tm, kp), lambda j, i: (i, 0)) for _ in a_parts]
    in_specs += [pl.BlockSpec((kp, tn), lambda j, i, part=part: (part, col0 // tn + j)) for part in range(n_parts)]
    args = list(a_parts) + [w] * n_parts
    if residual is not None:
        in_specs.append(pl.BlockSpec((tm, tn), lambda j, i: (i, j)))
        args.append(residual)
    return pl.pallas_call(
        functools.partial(_mm_kernel, n_parts=n_parts, has_residual=residual is not None),
        grid=(n // tn, m // tm),
        in_specs=in_specs,
        out_specs=pl.BlockSpec((tm, tn), lambda j, i: (i, j)),
        out_shape=jax.ShapeDtypeStruct((m, n), F32),
        scratch_shapes=[pltpu.VMEM((kp, tn), BF16) for _ in a_parts],
        compiler_params=_cparams(("parallel", "arbitrary")),
        name="matmul",
    )(*args)


def _ple_kernel(hn_ref, wg_ref, p_ref, wp_ref, h_ref, o_ref, wgb_ref):
    @pl.when(pl.program_id(1) == 0)
    def _():
        wgb_ref[...] = wg_ref[...].astype(BF16)

    gate = _sigmoid(_dot(hn_ref[...], wgb_ref[...]))
    proj = _dot(p_ref[...], wp_ref[...].astype(BF16))
    o_ref[...] = h_ref[...] + gate * proj


def ple(hn, wg, p, wp, h, layer):
    m, d = hn.shape
    n = wg.shape[2]
    pd = p.shape[1]
    tm = min(m, 1024)
    tn = min(n, 512)
    return pl.pallas_call(
        _ple_kernel,
        grid=(n // tn, m // tm),
        in_specs=[
            pl.BlockSpec((tm, d), lambda j, i: (i, 0)),
            pl.BlockSpec((None, d, tn), lambda j, i: (layer, 0, j)),
            pl.BlockSpec((tm, pd), lambda j, i: (i, 0)),
            pl.BlockSpec((None, pd, tn), lambda j, i: (layer, 0, j)),
            pl.BlockSpec((tm, tn), lambda j, i: (i, j)),
        ],
        out_specs=pl.BlockSpec((tm, tn), lambda j, i: (i, j)),
        out_shape=jax.ShapeDtypeStruct((m, n), F32),
        scratch_shapes=[pltpu.VMEM((d, tn), BF16)],
        compiler_params=_cparams(("parallel", "arbitrary")),
        name="ple",
    )(hn, wg, p, wp, h)


def _suffix_sum_matrix(c):
    return (_iota((c, c), 0) >= _iota((c, c), 1)).astype(BF16)


def _sb_prompt_kernel(q_ref, k_ref, v_ref, z_ref, o_ref, carry_ref, acc_ref, *, tq, tk):
    i = pl.program_id(2)
    qb = (q_ref[0] * HEAD_DIM ** -0.5).astype(BF16)
    incl_mat = _suffix_sum_matrix(tk)
    rc = _iota((tq, tk), 0) - _iota((tq, tk), 1)
    carry_ref[...] = jnp.zeros_like(carry_ref)
    acc_ref[...] = jnp.zeros_like(acc_ref)

    def chunk(j, masked):
        start = pl.multiple_of(j * tk, tk)
        k = k_ref[0, pl.ds(start, tk), :].astype(BF16)
        v = v_ref[0, pl.ds(start, tk), :].astype(BF16)
        z = _dot_nt(qb, k)
        lb = jnp.minimum(z, 0.0) - jnp.log(1.0 + jnp.exp(-jnp.abs(z)))
        lk = lb - z
        if masked:
            mask = rc > j * tk - i * tq
            lb = jnp.where(mask, lb, 0.0)
            lk = jnp.where(mask, lk, 0.0)
        hi = lk.astype(BF16)
        lo = (lk - hi.astype(F32)).astype(BF16)
        incl = _dot(hi, incl_mat) + _dot(lo, incl_mat)
        carry = carry_ref[...]
        a = jnp.exp(lb + (incl - lk + carry))
        if masked:
            a = jnp.where(mask, a, 0.0)
        acc_ref[...] += _dot(a.astype(BF16), v)
        carry_ref[...] = carry + incl[:, 0:1]

    n_diag = tq // tk
    top = (i + 1) * n_diag - 1
    for d in range(n_diag):
        chunk(top - d, True)

    def body(state):
        j, _ = state
        chunk(j, False)
        live = jnp.max(carry_ref[...]) > EXP_UNDERFLOW
        return j - 1, live.astype(jnp.int32)

    lax.while_loop(lambda st: (st[0] >= 0) & (st[1] > 0), body, (top - n_diag, jnp.int32(1)))
    o_ref[0] = (acc_ref[...] * _silu(z_ref[0])).astype(o_ref.dtype)


def sb_prompt(q, kv, z, *, tq=512, tk=256):
    b, t, _ = q.shape
    tq = min(tq, t)
    tk = min(tk, tq)
    hq = pl.BlockSpec((1, tq, HEAD_DIM), lambda bi, h, i: (bi, i, h))
    return pl.pallas_call(
        functools.partial(_sb_prompt_kernel, tq=tq, tk=tk),
        grid=(b, SB_HEADS, t // tq),
        in_specs=[
            hq,
            pl.BlockSpec((1, t, HEAD_DIM), lambda bi, h, i: (bi, 0, h)),
            pl.BlockSpec((1, t, HEAD_DIM), lambda bi, h, i: (bi, 0, SB_HEADS + h)),
            hq,
        ],
        out_specs=hq,
        out_shape=jax.ShapeDtypeStruct((b, t, SB_W), BF16),
        scratch_shapes=[pltpu.VMEM((tq, 1), F32), pltpu.VMEM((tq, HEAD_DIM), F32)],
        compiler_params=_cparams(("parallel", "parallel", "arbitrary")),
        name="sb_prompt",
    )(q, kv, kv, z)


def _pad_rows(x, rows):
    return jnp.concatenate([x, jnp.zeros((rows - x.shape[0], x.shape[1]), x.dtype)], axis=0)


def _sb_decode_kernel(pt_ref, q_ref, new_ref, z_ref, cache_ref, o_ref, buf_ref, sem_ref, carry_ref, acc_ref,
                      *, t, n_pages, page_off):
    b = pl.program_id(0)
    scale = HEAD_DIM ** -0.5
    rows = SB_HEADS * t
    tiles_per_key = 2 * SB_HEADS // SUBLANES
    tq = _iota((rows, 1), 0) % t
    col = _iota((1, PAGE_SIZE), 1)
    qb = q_ref[0].astype(BF16)

    def page_copies(slot, n):
        page = page_off + pt_ref[b, n_pages - 1 - n]
        return [pltpu.make_async_copy(cache_ref.at[page, :, pl.ds(r * SUBLANES, SUBLANES), :],
                                      buf_ref.at[slot, r], sem_ref.at[slot]) for r in range(tiles_per_key)]

    def attend(get_k, get_v, mask):
        z = jnp.concatenate([_dot_nt(qb[:, h * HEAD_DIM:(h + 1) * HEAD_DIM], get_k(h).astype(BF16))
                             for h in range(SB_HEADS)], axis=0) * scale
        carry = carry_ref[...]
        lb = jnp.minimum(z, 0.0) - jnp.log(1.0 + jnp.exp(-jnp.abs(z)))
        lk = lb - z
        if mask is not None:
            lb = jnp.where(mask, lb, 0.0)
            lk = jnp.where(mask, lk, 0.0)
        incl_mat = _suffix_sum_matrix(PAGE_SIZE)
        hi = lk.astype(BF16)
        lo = (lk - hi.astype(F32)).astype(BF16)
        incl = _dot(hi, incl_mat) + _dot(lo, incl_mat)
        a = jnp.exp(lb + (incl - lk + carry))
        if mask is not None:
            a = jnp.where(mask, a, 0.0)
        a = a.astype(BF16)
        for h in range(SB_HEADS):
            acc_ref[h * t:(h + 1) * t, :] += _dot(a[h * t:(h + 1) * t, :], get_v(h).astype(BF16))
        carry_ref[...] = carry + incl[:, 0:1]

    carry_ref[...] = jnp.zeros_like(carry_ref)
    acc_ref[...] = jnp.zeros_like(acc_ref)
    for cp in page_copies(0, 0):
        cp.start()
    new = _pad_rows(new_ref[0], PAGE_SIZE)
    attend(lambda h: new[:, h * HEAD_DIM:(h + 1) * HEAD_DIM],
           lambda h: new[:, SB_W + h * HEAD_DIM:SB_W + (h + 1) * HEAD_DIM], col < tq)

    def body(state):
        n, _ = state
        slot = n % 2
        for cp in page_copies(slot, n):
            cp.wait()

        @pl.when(n + 1 < n_pages)
        def _():
            for cp in page_copies(1 - slot, n + 1):
                cp.start()

        attend(lambda h: buf_ref[slot, h // SUBLANES, :, h % SUBLANES, :],
               lambda h: buf_ref[slot, (SB_HEADS + h) // SUBLANES, :, h % SUBLANES, :], None)
        live = jnp.max(carry_ref[...]) > EXP_UNDERFLOW
        return n + 1, live.astype(jnp.int32)

    n_done, _ = lax.while_loop(lambda st: (st[0] < n_pages) & (st[1] > 0), body, (jnp.int32(0), jnp.int32(1)))

    @pl.when(n_done < n_pages)
    def _():
        for cp in page_copies(n_done % 2, n_done):
            cp.wait()

    zg = z_ref[0]
    for h in range(SB_HEADS):
        sl = slice(h * HEAD_DIM, (h + 1) * HEAD_DIM)
        o_ref[0, :, sl] = (acc_ref[h * t:(h + 1) * t, :] * _silu(zg[:, sl])).astype(o_ref.dtype)


def sb_decode(q, new_kv, cache, page_off, page_table, z):
    b, t, _ = q.shape
    n_pages = page_table.shape[1]
    rows = SB_HEADS * t
    tiles_per_key = 2 * SB_HEADS // SUBLANES
    tok = lambda w: pl.BlockSpec((1, t, w), lambda bi, pt: (bi, 0, 0))
    grid_spec = pltpu.PrefetchScalarGridSpec(
        num_scalar_prefetch=1,
        grid=(b,),
        in_specs=[tok(SB_W), tok(2 * SB_W), tok(SB_W), pl.BlockSpec(memory_space=pl.ANY)],
        out_specs=tok(SB_W),
        scratch_shapes=[
            pltpu.VMEM((2, tiles_per_key, PAGE_SIZE, SUBLANES, HEAD_DIM), F32),
            pltpu.SemaphoreType.DMA((2,)),
            pltpu.VMEM((rows, 1), F32),
            pltpu.VMEM((rows, HEAD_DIM), F32),
        ],
    )
    return pl.pallas_call(
        functools.partial(_sb_decode_kernel, t=t, n_pages=n_pages, page_off=page_off),
        grid_spec=grid_spec,
        out_shape=jax.ShapeDtypeStruct((b, t, SB_W), BF16),
        compiler_params=_cparams(("arbitrary",)),
        name="sb_decode",
    )(page_table, q, new_kv, z, cache)


def _online_update(s, mask, m, l):
    sm = jnp.where(mask, s, NEG)
    m_new = jnp.maximum(m, jnp.max(sm, axis=1, keepdims=True))
    alpha = jnp.exp(m - m_new)
    p = jnp.where(mask, jnp.exp(sm - m_new), 0.0)
    l_new = alpha * l + jnp.sum(p, axis=1, keepdims=True)
    return p, alpha, m_new, l_new


def _flash_step(s, v, m_ref, l_ref, acc_ref):
    m_old = m_ref[...]
    m_new = jnp.maximum(m_old, jnp.max(s, axis=1, keepdims=True))
    alpha = jnp.exp(m_old - m_new)
    p = jnp.exp(s - m_new)
    l_ref[...] = alpha * l_ref[...] + jnp.sum(p, axis=1, keepdims=True)
    m_ref[...] = m_new
    acc_ref[...] = alpha * acc_ref[...] + _dot(p.astype(BF16), v)


def _flash_step_t(s_t, v_t, m_ref, l_ref, acc_ref):
    m_old = m_ref[...]
    m_new = jnp.maximum(m_old, jnp.max(s_t, axis=0, keepdims=True))
    alpha = jnp.exp(m_old - m_new)
    p = jnp.exp(s_t - m_new)
    l_ref[...] = alpha * l_ref[...] + jnp.sum(p, axis=0, keepdims=True)
    m_ref[...] = m_new
    acc_ref[...] = alpha * acc_ref[...] + _dot(v_t, p.astype(BF16))


def _pipelined_sweep(n, scores, fold):
    s = scores(0, True)

    def body(c, s):
        s_next = scores(c + 1, False)
        fold(s, c)
        return s_next

    s = lax.fori_loop(0, n - 1, body, s)

    @pl.when(n >= 1)
    def _():
        s_last = scores(n, True)
        fold(s, n - 1)
        fold(s_last, n)

    @pl.when(n == 0)
    def _():
        fold(s, 0)


def _normalise(acc, l):
    return acc / jnp.maximum(l, 1e-30)


def _diff_lambda(lq1_ref, lk1_ref, lq2_ref, lk2_ref, lambda_init):
    d1 = jnp.sum(lq1_ref[...] * lk1_ref[...], axis=1, keepdims=True)
    d2 = jnp.sum(lq2_ref[...] * lk2_ref[...], axis=1, keepdims=True)
    return jnp.exp(d1) - jnp.exp(d2) + lambda_init


def _diff_finish(o, hg, zg, lambda_init):
    ms = jnp.mean(o * o, axis=-1, keepdims=True)
    o = o * lax.rsqrt(ms + EPS) * hg * (1.0 - lambda_init)
    return o * _silu(zg)


def _diff_prompt_kernel(slopes_ref, q_ref, k_ref, v_ref, z_ref, lq1_ref, lk1_ref, lq2_ref, lk2_ref, hg_ref, o_ref,
                        vt_ref, m_ref, l_ref, acc_ref, *, tq, tk, lambda_init):
    h = pl.program_id(1)
    i = pl.program_id(2)
    slope = slopes_ref[h]
    t_total = v_ref.shape[1]

    @pl.when(i == 0)
    def _():
        for c in range(t_total // tk):
            vt_ref[c] = v_ref[0, c * tk:(c + 1) * tk, :].T.astype(BF16)

    q = q_ref[0] * DIFF_HALF ** -0.5
    qs = [q[:, :DIFF_HALF].astype(BF16), q[:, DIFF_HALF:].astype(BF16)]
    rc = _iota((tk, tq), 1) - _iota((tk, tq), 0)
    bias_rc = slope * rc.astype(F32)
    m_ref[...] = jnp.full_like(m_ref, NEG)
    l_ref[...] = jnp.zeros_like(l_ref)
    acc_ref[...] = jnp.zeros_like(acc_ref)

    def scores(j, edge):
        start = pl.multiple_of(j * tk, tk)
        k = k_ref[0, pl.ds(start, tk), :]
        off = (i - j) * tk
        bias = bias_rc + slope * off.astype(F32)
        out = []
        for c in range(2):
            s = _dot_nt(k[:, c * DIFF_HALF:(c + 1) * DIFF_HALF].astype(BF16), qs[c]) - bias
            out.append(jnp.where(rc + off >= 0, s, NEG) if edge else s)
        return tuple(out)

    def fold(s, j):
        vt = vt_ref[j]
        for c in range(2):
            _flash_step_t(s[c], vt, m_ref.at[c], l_ref.at[c], acc_ref.at[c])

    _pipelined_sweep(i, scores, fold)
    lam = _diff_lambda(lq1_ref, lk1_ref, lq2_ref, lk2_ref, lambda_init)
    o_t = _normalise(acc_ref[0], l_ref[0]) - lam * _normalise(acc_ref[1], l_ref[1])
    o_ref[0] = _diff_finish(o_t.T, hg_ref[...], z_ref[0], lambda_init).astype(o_ref.dtype)


def _smem_spec():
    return pl.BlockSpec(memory_space=pltpu.SMEM)


def diff_prompt(q, kv, z, lq1, lk1, lq2, lk2, head_g, lambda_init, *, tq=256, tk=256):
    b, t, _ = q.shape
    tq = min(tq, t)
    tk = min(tk, tq)
    assert tq == tk and t % tq == 0
    hq = pl.BlockSpec((1, tq, DIFF_VDIM), lambda bi, h, i: (bi, i, h))
    vec = lambda w: pl.BlockSpec((1, w), lambda bi, h, i: (0, 0))
    return pl.pallas_call(
        functools.partial(_diff_prompt_kernel, tq=tq, tk=tk, lambda_init=lambda_init),
        grid=(b, DIFF_HEADS, t // tq),
        in_specs=[
            _smem_spec(),
            hq,
            pl.BlockSpec((1, t, DIFF_VDIM), lambda bi, h, i: (bi, 0, h)),
            pl.BlockSpec((1, t, DIFF_VDIM), lambda bi, h, i: (bi, 0, DIFF_HEADS + h)),
            hq,
            vec(DIFF_HALF), vec(DIFF_HALF), vec(DIFF_HALF), vec(DIFF_HALF), vec(DIFF_VDIM),
        ],
        out_specs=hq,
        out_shape=jax.ShapeDtypeStruct((b, t, C_W), BF16),
        scratch_shapes=[pltpu.VMEM((t // tk, DIFF_VDIM, tk), BF16),
                        pltpu.VMEM((2, 1, tq), F32), pltpu.VMEM((2, 1, tq), F32),
                        pltpu.VMEM((2, DIFF_VDIM, tq), F32)],
        compiler_params=_cparams(("parallel", "parallel", "arbitrary")),
        name="diff_prompt",
    )(jnp.asarray(_alibi_slopes(DIFF_HEADS)), q, kv, kv, z,
      lq1.reshape(1, -1), lk1.reshape(1, -1), lq2.reshape(1, -1), lk2.reshape(1, -1), head_g.reshape(1, -1))


DIFF_PAGES_PER_STEP = 4


def _diff_decode_kernel(pt_ref, q_ref, knew_ref, vnew_ref, *refs, t, n_pages, pps, lambda_init):
    page_refs = refs[:2 * pps]
    (z_ref, slope_ref, lq1_ref, lk1_ref, lq2_ref, lk2_ref, hg_ref, o_ref, qs_ref, bias_ref, biasn_ref, m_ref,
     l_ref, acc_ref) = refs[2 * pps:]
    p = pl.program_id(1)
    n_steps = n_pages // pps
    scale = DIFF_HALF ** -0.5
    n_tiles = DIFF_HEADS // SUBLANES
    hr = SUBLANES * t
    past = n_pages * PAGE_SIZE

    def tile_bias(a, n_keys, causal):
        shape = (hr, n_keys * SUBLANES)
        r, c = _iota(shape, 0), _iota(shape, 1)
        ok = (r // t) == (c % SUBLANES)
        rel = r % t - c // SUBLANES
        if causal:
            ok = ok & (rel >= 0)
        return jnp.where(ok, -slope_ref[a] * rel.astype(F32), NEG)

    def attend(a, get_k, v, bias):
        s = jnp.concatenate([_dot_nt(qs_ref[c, a], get_k(c).astype(BF16)) + bias for c in range(2)], axis=0)
        m_old = m_ref[a]
        m_new = jnp.maximum(m_old, jnp.max(s, axis=1, keepdims=True))
        alpha = jnp.exp(m_old - m_new)
        pr = jnp.exp(s - m_new)
        l_ref[a] = alpha * l_ref[a] + jnp.sum(pr, axis=1, keepdims=True)
        m_ref[a] = m_new
        acc_ref[a] = alpha * acc_ref[a] + _dot(pr.astype(BF16), v.astype(BF16))

    @pl.when(p == 0)
    def _():
        q = q_ref[0] * scale
        for a in range(n_tiles):
            for c in range(2):
                cols = [(a * SUBLANES + hl) * DIFF_VDIM + c * DIFF_HALF for hl in range(SUBLANES)]
                qs_ref[c, a] = jnp.concatenate([q[:, o:o + DIFF_HALF] for o in cols], axis=0).astype(BF16)
            bias_ref[a] = tile_bias(a, PAGE_SIZE, False)
            biasn_ref[a] = tile_bias(a, t, True)
        m_ref[...] = jnp.full_like(m_ref, NEG)
        l_ref[...] = jnp.zeros_like(l_ref)
        acc_ref[...] = jnp.zeros_like(acc_ref)

    @pl.when(p < n_steps)
    def _():
        for k in range(pps):
            kpage_ref, vpage_ref = page_refs[2 * k], page_refs[2 * k + 1]
            base = (past - (p * pps + k) * PAGE_SIZE).astype(F32)
            for a in range(n_tiles):
                rows = slice(a * SUBLANES, (a + 1) * SUBLANES)
                get_k = lambda c: kpage_ref[:, rows, c * DIFF_HALF:(c + 1) * DIFF_HALF].reshape(
                    PAGE_SIZE * SUBLANES, DIFF_HALF)
                v = vpage_ref[:, rows, :].reshape(PAGE_SIZE * SUBLANES, DIFF_VDIM)
                attend(a, get_k, v, bias_ref[a] - slope_ref[a] * base)

    @pl.when(p == n_steps)
    def _():
        lam = _diff_lambda(lq1_ref, lk1_ref, lq2_ref, lk2_ref, lambda_init)
        zg = z_ref[0]
        hg = hg_ref[...]
        for a in range(n_tiles):
            rows = slice(a * SUBLANES, (a + 1) * SUBLANES)
            get_k = lambda c: knew_ref[0, :, rows, c * DIFF_HALF:(c + 1) * DIFF_HALF].reshape(t * SUBLANES, DIFF_HALF)
            attend(a, get_k, vnew_ref[0, :, rows, :].reshape(t * SUBLANES, DIFF_VDIM), biasn_ref[a])
            o = _normalise(acc_ref[a], l_ref[a])
            for hl in range(SUBLANES):
                r1 = slice(hl * t, (hl + 1) * t)
                r2 = slice(hr + hl * t, hr + (hl + 1) * t)
                sl = slice((a * SUBLANES + hl) * DIFF_VDIM, (a * SUBLANES + hl + 1) * DIFF_VDIM)
                o_ref[0, :, sl] = _diff_finish(o[r1] - lam * o[r2], hg, zg[:, sl], lambda_init).astype(o_ref.dtype)


def diff_decode(q, new_kv, cache, page_off, page_table, z, lq1, lk1, lq2, lk2, head_g, lambda_init):
    b, t, _ = q.shape
    n_pages = page_table.shape[1]
    n_tiles = DIFF_HEADS // SUBLANES
    hr = SUBLANES * t
    slopes = np.repeat(_alibi_slopes(DIFF_HEADS), t).reshape(n_tiles, hr, 1)
    tok = lambda w: pl.BlockSpec((1, t, w), lambda bi, p, pt: (bi, 0, 0))
    vec = lambda w: pl.BlockSpec((1, w), lambda bi, p, pt: (0, 0))
    new_spec = lambda kv: pl.BlockSpec((1, t, DIFF_HEADS, DIFF_VDIM), lambda bi, p, pt: (bi, 0, kv, 0))
    pps = math.gcd(n_pages, DIFF_PAGES_PER_STEP)
    n_steps = n_pages // pps
    page_spec = lambda k, kv: pl.BlockSpec(
        (None, PAGE_SIZE, DIFF_HEADS, DIFF_VDIM),
        lambda bi, p, pt: (page_off + pt[bi, jnp.minimum(p, n_steps - 1) * pps + k], 0, kv, 0))
    grid_spec = pltpu.PrefetchScalarGridSpec(
        num_scalar_prefetch=1,
        grid=(b, n_steps + 1),
        in_specs=[
            tok(C_W), new_spec(0), new_spec(1),
            *[page_spec(k, kv) for k in range(pps) for kv in range(2)],
            tok(C_W),
            pl.BlockSpec((n_tiles, hr, 1), lambda bi, p, pt: (0, 0, 0)),
            vec(DIFF_HALF), vec(DIFF_HALF), vec(DIFF_HALF), vec(DIFF_HALF), vec(DIFF_VDIM),
        ],
        out_specs=tok(C_W),
        scratch_shapes=[
            pltpu.VMEM((2, n_tiles, hr, DIFF_HALF), BF16),
            pltpu.VMEM((n_tiles, hr, PAGE_SIZE * SUBLANES), F32),
            pltpu.VMEM((n_tiles, hr, t * SUBLANES), F32),
            pltpu.VMEM((n_tiles, 2 * hr, 1), F32),
            pltpu.VMEM((n_tiles, 2 * hr, 1), F32),
            pltpu.VMEM((n_tiles, 2 * hr, DIFF_VDIM), F32),
        ],
    )
    return pl.pallas_call(
        functools.partial(_diff_decode_kernel, t=t, n_pages=n_pages, pps=pps, lambda_init=lambda_init),
        grid_spec=grid_spec,
        out_shape=jax.ShapeDtypeStruct((b, t, C_W), BF16),
        compiler_params=_cparams(("parallel", "arbitrary")),
        name="diff_decode",
    )(page_table, q, new_kv, new_kv, *([cache] * (2 * pps)), z, jnp.asarray(slopes),
      lq1.reshape(1, -1), lk1.reshape(1, -1), lq2.reshape(1, -1), lk2.reshape(1, -1), head_g.reshape(1, -1))


def _masked_softmax_parts(parts):
    m = None
    for s, mask in parts:
        pm = jnp.max(jnp.where(mask, s, NEG), axis=1, keepdims=True)
        m = pm if m is None else jnp.maximum(m, pm)
    es = [jnp.where(mask, jnp.exp(jnp.where(mask, s, NEG) - m), 0.0) for s, mask in parts]
    den = None
    for e in es:
        d = jnp.sum(e, axis=1, keepdims=True)
        den = d if den is None else den + d
    den = jnp.maximum(den, 1e-30)
    return [e / den for e in es]


def _nsa_prompt_kernel(slopes_ref, q_ref, kc_ref, vc_ref, ks_ref, vs_ref, kw_ref, vw_ref, lg_ref, z_ref,
                       wk_ref, wv_ref, expand_ref, o_ref, kcmp_ref, vcmp_ref, vst_ref, vwt_ref, selbias_ref, m_ref,
                       l_ref, acc_ref, *, tq, t_total):
    g = pl.program_id(1)
    i = pl.program_id(2)
    scale = HEAD_DIM ** -0.5
    nb = t_total // SEL_BLOCK
    rows = NSA_GROUP * tq
    tk = tq

    @pl.when(i == 0)
    def _():
        for src, w_ref, dst in ((kc_ref, wk_ref, kcmp_ref), (vc_ref, wv_ref, vcmp_ref)):
            x = src[0].reshape(nb, SEL_BLOCK, HEAD_DIM)
            w = w_ref[0][None]
            even = jnp.sum(x[:, :CMP_BLOCK, :] * w, axis=1)
            odd = jnp.sum(x[:, CMP_BLOCK:, :] * w, axis=1)
            pad = jnp.zeros((LANES - 2 * nb, HEAD_DIM), F32)
            dst[...] = jnp.concatenate([even, odd, pad], axis=0).astype(BF16)
        for src, dst in ((vs_ref, vst_ref), (vw_ref, vwt_ref)):
            for c in range(t_total // tk):
                dst[c] = src[0, c * tk:(c + 1) * tk, :].T.astype(BF16)

    q = q_ref[0] * scale
    q4 = jnp.concatenate([q[:, zz * HEAD_DIM:(zz + 1) * HEAD_DIM] for zz in range(NSA_GROUP)], axis=0).astype(BF16)
    qpos1 = i * tq + _iota((tq, 1), 0)
    qpos4 = jnp.concatenate([qpos1] * NSA_GROUP, axis=0)
    slope4 = jnp.concatenate(
        [jnp.full((tq, 1), slopes_ref[g * NSA_GROUP + zz], F32) for zz in range(NSA_GROUP)], axis=0)
    col = _iota((1, LANES), 1)

    cidx = jnp.where(col < nb, 2 * col, 2 * (col - nb) + 1)
    dist_c = qpos4 - (cidx * CMP_BLOCK + (CMP_BLOCK - 1))
    s_c = _dot_nt(q4, kcmp_ref[...]) - slope4 * dist_c.astype(F32)
    (p_c,) = _masked_softmax_parts([(s_c, (dist_c >= 0) & (col < 2 * nb))])
    o_cmp = _dot(p_c.astype(BF16), vcmp_ref[...])
    imp = p_c[0:tq]
    for zz in range(1, NSA_GROUP):
        imp = imp + p_c[zz * tq:(zz + 1) * tq]
    pair = imp + pltpu.roll(imp, LANES - nb, 1)

    cur = qpos1 // SEL_BLOCK
    valid = col * SEL_BLOCK <= qpos1
    forced = (col == cur) | (col == 0)
    score = jnp.where(forced, FORCED_SCORE, jnp.where(valid, pair, -1.0))
    score = jnp.where(col < nb, score, -2.0)
    score_t = score.T
    cand = score_t[:nb]
    blk = _iota((nb, 1), 0)
    rank = jnp.zeros((nb, tq), F32)
    for j in range(nb):
        r = score_t[j:j + 1, :]
        ge = jnp.where(r >= cand, 1.0, 0.0)
        gt = jnp.where(r > cand, 1.0, 0.0)
        rank = rank + jnp.where(blk > j, ge, gt)
    sel_t = (rank < float(min(SEL_TOPK, nb))).astype(F32)
    sel_rows = jnp.concatenate([sel_t, jnp.zeros((LANES - nb, tq), F32)], axis=0).astype(BF16)
    selbias_ref[...] = (_dot(expand_ref[...], sel_rows) - 1.0) * (-NEG)

    tile = lambda x: jnp.concatenate([x] * NSA_GROUP, axis=1)
    rc4 = tile(_iota((tk, tq), 1) - _iota((tk, tq), 0))
    slope_row = jnp.concatenate(
        [jnp.full((1, tq), slopes_ref[g * NSA_GROUP + zz], F32) for zz in range(NSA_GROUP)], axis=1)
    bias_rc = slope_row * rc4.astype(F32)

    def reset():
        m_ref[...] = jnp.full_like(m_ref, NEG)
        l_ref[...] = jnp.zeros_like(l_ref)
        acc_ref[...] = jnp.zeros_like(acc_ref)

    def scores(k_ref, j, admit):
        start = pl.multiple_of(j * tk, tk)
        k = k_ref[0, pl.ds(start, tk), :].astype(BF16)
        off = (i - j) * tk
        s = _dot_nt(k, q4) - (bias_rc + slope_row * off.astype(F32))
        return admit(s, start, rc4 + off)

    def slc_admit(edge):
        def admit(s, start, dist):
            s = s + tile(selbias_ref[pl.ds(start, tk), :])
            return jnp.where(dist >= 0, s, NEG) if edge else s
        return admit

    reset()
    _pipelined_sweep(i, lambda c, edge: scores(ks_ref, c, slc_admit(edge)),
                     lambda s, c: _flash_step_t(s, vst_ref[c], m_ref, l_ref, acc_ref))
    o_slc_t = _normalise(acc_ref[...], l_ref[...])

    def win_admit(edge):
        return (lambda s, start, dist: jnp.where((dist >= 0) & (dist < WINDOW), s, NEG)) if edge else (
            lambda s, start, dist: s)

    reset()
    _pipelined_sweep(jnp.minimum(i, WINDOW // tk), lambda c, edge: scores(kw_ref, i - c, win_admit(edge)),
                     lambda s, c: _flash_step_t(s, vwt_ref[i - c], m_ref, l_ref, acc_ref))
    o_win_t = _normalise(acc_ref[...], l_ref[...])

    gates = pltpu.roll(_sigmoid(lg_ref[0]), (LANES - 3 * NSA_GROUP * g) % LANES, 1)
    zg = z_ref[0]
    for zz in range(NSA_GROUP):
        r = slice(zz * tq, (zz + 1) * tq)
        mix = (gates[:, 3 * zz:3 * zz + 1] * o_cmp[r] + gates[:, 3 * zz + 1:3 * zz + 2] * o_slc_t[:, r].T
               + gates[:, 3 * zz + 2:3 * zz + 3] * o_win_t[:, r].T)
        sl = slice(zz * HEAD_DIM, (zz + 1) * HEAD_DIM)
        o_ref[0, :, sl] = (mix * _silu(zg[:, sl])).astype(o_ref.dtype)


def _cmp_weight_rows(w):
    return jnp.broadcast_to(w.T[:, :, None], (NSA_KV_HEADS, CMP_BLOCK, HEAD_DIM)).astype(F32)


def nsa_prompt(q, nskv, win, logits, z, cmp_wk, cmp_wv, *, tq=128):
    b, t, _ = q.shape
    nb = t // SEL_BLOCK
    assert t % tq == 0 and 2 * nb <= LANES and WINDOW % tq == 0
    gw = NSA_GROUP * HEAD_DIM
    expand = ((np.arange(t)[:, None] // SEL_BLOCK) == np.arange(LANES)[None, :]).astype(np.float32)
    kvspec = lambda kind: pl.BlockSpec((1, t, HEAD_DIM), lambda bi, g, i: (bi, 0, kind * NSA_KV_HEADS + g))
    qspec = pl.BlockSpec((1, tq, gw), lambda bi, g, i: (bi, i, g))
    wspec = pl.BlockSpec((1, CMP_BLOCK, HEAD_DIM), lambda bi, g, i: (g, 0, 0))
    return pl.pallas_call(
        functools.partial(_nsa_prompt_kernel, tq=tq, t_total=t),
        grid=(b, NSA_KV_HEADS, t // tq),
        in_specs=[
            _smem_spec(),
            qspec,
            kvspec(0), kvspec(1), kvspec(2), kvspec(3),
            kvspec(0), kvspec(1),
            pl.BlockSpec((1, tq, LANES), lambda bi, g, i: (bi, i, 0)),
            qspec,
            wspec, wspec,
            pl.BlockSpec((t, LANES), lambda bi, g, i: (0, 0)),
        ],
        out_specs=qspec,
        out_shape=jax.ShapeDtypeStruct((b, t, NSA_W), BF16),
        scratch_shapes=[
            pltpu.VMEM((LANES, HEAD_DIM), BF16),
            pltpu.VMEM((LANES, HEAD_DIM), BF16),
            pltpu.VMEM((t // tq, HEAD_DIM, tq), BF16),
            pltpu.VMEM((t // tq, HEAD_DIM, tq), BF16),
            pltpu.VMEM((t, tq), F32),
            pltpu.VMEM((1, NSA_GROUP * tq), F32),
            pltpu.VMEM((1, NSA_GROUP * tq), F32),
            pltpu.VMEM((HEAD_DIM, NSA_GROUP * tq), F32),
        ],
        compiler_params=_cparams(("parallel", "parallel", "arbitrary")),
        name="nsa_prompt",
    )(jnp.asarray(_alibi_slopes(NSA_HEADS)), q, nskv, nskv, nskv, nskv, win, win, logits, z,
      _cmp_weight_rows(cmp_wk), _cmp_weight_rows(cmp_wv), jnp.asarray(expand, BF16))


NSA_PAGES_PER_STEP = 4


def _nsa_compress_kernel(pt_ref, *refs):
    page_refs, w_ref, even_ref, odd_ref = refs[:-3], refs[-3], refs[-2], refs[-1]
    n_cmp = PAGE_SIZE // CMP_BLOCK
    w = w_ref[...][None]
    evens, odds = [], []
    for page_ref in page_refs:
        x = page_ref[...].reshape(n_cmp, CMP_BLOCK, 2 * NSA_KV_HEADS, HEAD_DIM)
        c = jnp.sum(x * w, axis=1)
        evens += [c[r:r + 1] for r in range(0, n_cmp, 2)]
        odds += [c[r:r + 1] for r in range(1, n_cmp, 2)]
    even_ref[0] = jnp.concatenate(evens, axis=0)
    odd_ref[0] = jnp.concatenate(odds, axis=0)


def nsa_compress_pages(cache, page_off, page_table, cmp_wk, cmp_wv):
    b, n_pages = page_table.shape
    pps = math.gcd(n_pages, NSA_PAGES_PER_STEP)
    half_blocks = PAGE_SIZE // CMP_BLOCK // 2
    w = jnp.concatenate([cmp_wk, cmp_wv], axis=1).astype(F32)
    w = jnp.broadcast_to(w[:, :, None], (CMP_BLOCK, 2 * NSA_KV_HEADS, HEAD_DIM))
    cmp_spec = pl.BlockSpec((1, pps * half_blocks, 2 * NSA_KV_HEADS, HEAD_DIM), lambda bi, p, pt: (bi, p, 0, 0))
    page_spec = lambda k: pl.BlockSpec((None, PAGE_SIZE, 2 * NSA_KV_HEADS, HEAD_DIM),
                                       lambda bi, p, pt: (page_off + pt[bi, p * pps + k], 0, 0, 0))
    grid_spec = pltpu.PrefetchScalarGridSpec(
        num_scalar_prefetch=1,
        grid=(b, n_pages // pps),
        in_specs=[page_spec(k) for k in range(pps)]
        + [pl.BlockSpec((CMP_BLOCK, 2 * NSA_KV_HEADS, HEAD_DIM), lambda bi, p, pt: (0, 0, 0))],
        out_specs=[cmp_spec, cmp_spec],
    )
    return pl.pallas_call(
        _nsa_compress_kernel,
        grid_spec=grid_spec,
        out_shape=[jax.ShapeDtypeStruct((b, n_pages * half_blocks, 2 * NSA_KV_HEADS, HEAD_DIM), F32)] * 2,
        compiler_params=_cparams(("parallel", "parallel")),
        name="nsa_compress_pages",
    )(page_table, *([cache] * pps), w)


def _nsa_decode_kernel(pt_ref, q_ref, even_ref, odd_ref, *refs, t, n_pages, pps):
    page_refs = refs[:pps]
    (new_ref, wst_ref, wnew_ref, lg_ref, z_ref, slope_ref, o_ref, qg_ref, sel_ref, ocmp_ref, m_ref, l_ref,
     acc_ref) = refs[pps:]
    p = pl.program_id(1)
    n_steps = n_pages // pps
    scale = HEAD_DIM ** -0.5
    rows = NSA_HEADS * t
    grp_rows = NSA_GROUP * t
    past = n_pages * PAGE_SIZE
    n_blk = past // SEL_BLOCK
    half = n_blk
    tq = _iota((rows, 1), 0) % t
    qpos = past + tq
    slope = slope_ref[...]
    col = _iota((1, PAGE_SIZE), 1)

    def scores(get_k):
        return jnp.concatenate([_dot_nt(qg_ref[gg], get_k(gg).astype(BF16)) for gg in range(NSA_KV_HEADS)],
                               axis=0) * scale

    def group_pv(pr, get_v):
        return jnp.concatenate([_dot(pr[gg * grp_rows:(gg + 1) * grp_rows], get_v(gg).astype(BF16))
                                for gg in range(NSA_KV_HEADS)], axis=0)

    def rows_kv(ref3):
        return (lambda gg: ref3[:, gg, :]), (lambda gg: ref3[:, NSA_KV_HEADS + gg, :])

    def lanes_kv(x, first_block):
        blk = lambda i: x[:, (first_block + i) * HEAD_DIM:(first_block + i + 1) * HEAD_DIM]
        return (lambda gg: blk(gg)), (lambda gg: blk(NSA_KV_HEADS + gg))

    @pl.when(p == 0)
    def _():
        q = q_ref[0]
        for gg in range(NSA_KV_HEADS):
            heads = range(gg * NSA_GROUP, (gg + 1) * NSA_GROUP)
            qg_ref[gg] = jnp.concatenate([q[:, h * HEAD_DIM:(h + 1) * HEAD_DIM] for h in heads], axis=0).astype(BF16)
        parts, vals = [], []
        for par, cmp_ref in enumerate((even_ref, odd_ref)):
            get_k, get_v = rows_kv(cmp_ref.at[0])
            vals.append(get_v)
            c_end = (2 * _iota((1, half), 1) + par) * CMP_BLOCK + (CMP_BLOCK - 1)
            dist = qpos - c_end
            parts.append((scores(get_k) - slope * dist.astype(F32), dist >= 0))
        p_e, p_o = _masked_softmax_parts(parts)
        ocmp_ref[...] = group_pv(p_e.astype(BF16), vals[0]) + group_pv(p_o.astype(BF16), vals[1])
        pe = p_e + p_o
        blkcol = _iota((1, n_blk), 1)
        picked = []
        for gg in range(NSA_KV_HEADS):
            imp = pe[gg * grp_rows:gg * grp_rows + t]
            for zz in range(1, NSA_GROUP):
                imp = imp + pe[gg * grp_rows + zz * t:gg * grp_rows + (zz + 1) * t]
            sc = jnp.where(blkcol == 0, -1.0, imp)
            sel = (blkcol == 0)
            for _ in range(min(SEL_TOPK, n_blk + 1) - 2):
                mx = jnp.max(sc, axis=1, keepdims=True)
                first = jnp.min(jnp.where(sc == mx, blkcol, n_blk), axis=1, keepdims=True)
                hit = blkcol == first
                sel = sel | hit
                sc = jnp.where(hit, -1.0, sc)
            self32 = sel.astype(F32)
            picked.extend([self32] * NSA_GROUP)
        sel_ref[...] = jnp.concatenate(picked, axis=0).astype(BF16)
        m_ref[...] = jnp.full_like(m_ref, NEG)
        l_ref[...] = jnp.zeros_like(l_ref)
        acc_ref[...] = jnp.zeros_like(acc_ref)

    def attend(get_k, get_v, dist, mask):
        s = scores(get_k) - slope * dist.astype(F32)
        pr, alpha, m_new, l_new = _online_update(s, mask, m_ref[...], l_ref[...])
        m_ref[...] = m_new
        l_ref[...] = l_new
        acc_ref[...] = alpha * acc_ref[...] + group_pv(pr.astype(BF16), get_v)

    @pl.when(p < n_steps)
    def _():
        keys = pps * PAGE_SIZE
        blocks_per_step = keys // SEL_BLOCK
        expand = (_iota((n_blk, keys), 0) == blocks_per_step * p + _iota((n_blk, keys), 1) // SEL_BLOCK)
        chosen = _dot(sel_ref[...], expand.astype(BF16)) > 0.5
        dist = qpos - (p * keys + _iota((1, keys), 1))
        getters = [rows_kv(page_ref) for page_ref in page_refs]
        get_k = lambda gg: jnp.concatenate([gk(gg) for gk, _ in getters], axis=0)
        get_v = lambda gg: jnp.concatenate([gv(gg) for _, gv in getters], axis=0)
        attend(get_k, get_v, dist, chosen & (dist >= 0))

    @pl.when(p == n_steps)
    def _():
        new = _pad_rows(new_ref[0], PAGE_SIZE)
        dist = qpos - (past + col)
        get_k, get_v = lanes_kv(new, 2 * NSA_KV_HEADS)
        attend(get_k, get_v, dist, (dist >= 0) & (col < t))
        o_slc = _normalise(acc_ref[...], l_ref[...])
        n_state = wst_ref.shape[1]
        dist_s = qpos - (past - n_state + _iota((1, n_state), 1))
        ks_state, vs_state = rows_kv(wst_ref.at[0])
        ks_new, vs_new = lanes_kv(_pad_rows(wnew_ref[0], PAGE_SIZE), 0)
        p_s, p_n = _masked_softmax_parts([
            (scores(ks_state) - slope * dist_s.astype(F32), (dist_s >= 0) & (dist_s < WINDOW)),
            (scores(ks_new) - slope * dist.astype(F32), (dist >= 0) & (dist < WINDOW) & (col < t)),
        ])
        o_win = group_pv(p_s.astype(BF16), vs_state) + group_pv(p_n.astype(BF16), vs_new)
        o_cmp = ocmp_ref[...]
        gates = _sigmoid(lg_ref[0])
        zg = z_ref[0]
        for h in range(NSA_HEADS):
            r = slice(h * t, (h + 1) * t)
            mix = (gates[:, 3 * h:3 * h + 1] * o_cmp[r] + gates[:, 3 * h + 1:3 * h + 2] * o_slc[r]
                   + gates[:, 3 * h + 2:3 * h + 3] * o_win[r])
            sl = slice(h * HEAD_DIM, (h + 1) * HEAD_DIM)
            o_ref[0, :, sl] = (mix * _silu(zg[:, sl])).astype(o_ref.dtype)


def nsa_decode(q, cmp_even, cmp_odd, cache, page_off, page_table, new_nskv, win_state, win_new, logits, z):
    b, t, _ = q.shape
    n_pages = page_table.shape[1]
    past = n_pages * PAGE_SIZE
    n_blk = past // SEL_BLOCK
    n_state = win_state.shape[1]
    assert past % SEL_BLOCK == 0 and t < CMP_BLOCK and n_blk + 1 >= SEL_TOPK and n_state >= WINDOW - 1
    rows = NSA_HEADS * t
    slopes = np.repeat(_alibi_slopes(NSA_HEADS), t).reshape(rows, 1)
    pps = math.gcd(n_pages, NSA_PAGES_PER_STEP)
    n_steps = n_pages // pps
    tok = lambda w: pl.BlockSpec((1, t, w), lambda bi, p, pt: (bi, 0, 0))
    page_spec = lambda k: pl.BlockSpec(
        (None, PAGE_SIZE, 2 * NSA_KV_HEADS, HEAD_DIM),
        lambda bi, p, pt: (page_off + pt[bi, jnp.minimum(p, n_steps - 1) * pps + k], 0, 1, 0))
    grid_spec = pltpu.PrefetchScalarGridSpec(
        num_scalar_prefetch=1,
        grid=(b, n_steps + 1),
        in_specs=[
            tok(NSA_W),
            pl.BlockSpec((1, n_blk, 2 * NSA_KV_HEADS, HEAD_DIM), lambda bi, p, pt: (bi, 0, 0, 0)),
            pl.BlockSpec((1, n_blk, 2 * NSA_KV_HEADS, HEAD_DIM), lambda bi, p, pt: (bi, 0, 0, 0)),
            *[page_spec(k) for k in range(pps)],
            tok(4 * NSA_KV_W),
            pl.BlockSpec((1, n_state, 2 * NSA_KV_HEADS, HEAD_DIM), lambda bi, p, pt: (bi, 0, 0, 0)),
            tok(2 * NSA_KV_W),
            tok(LANES),
            tok(NSA_W),
            pl.BlockSpec((rows, 1), lambda bi, p, pt: (0, 0)),
        ],
        out_specs=tok(NSA_W),
        scratch_shapes=[
            pltpu.VMEM((NSA_KV_HEADS, NSA_GROUP * t, HEAD_DIM), BF16),
            pltpu.VMEM((rows, n_blk), BF16),
            pltpu.VMEM((rows, HEAD_DIM), F32),
            pltpu.VMEM((rows, 1), F32),
            pltpu.VMEM((rows, 1), F32),
            pltpu.VMEM((rows, HEAD_DIM), F32),
        ],
    )
    return pl.pallas_call(
        functools.partial(_nsa_decode_kernel, t=t, n_pages=n_pages, pps=pps),
        grid_spec=grid_spec,
        out_shape=jax.ShapeDtypeStruct((b, t, NSA_W), BF16),
        compiler_params=_cparams(("parallel", "arbitrary")),
        name="nsa_decode",
    )(page_table, q, cmp_even, cmp_odd, *([cache] * pps), new_nskv, win_state, win_new, logits, z, jnp.asarray(slopes))


def _ab_projections(hn, w_in):
    o = np.cumsum([0, SB_W, 2 * SB_W, SB_W, NSA_W, 4 * NSA_KV_W, 2 * NSA_KV_W, 3 * NSA_HEADS, NSA_W])
    names = ("sbq", "sbkv", "sbz", "nsq", "nskv", "win")
    proj = {name: matmul([hn], w_in, col0=int(o[c]), n=int(o[c + 1] - o[c])) for c, name in enumerate(names)}
    proj["logits"] = matmul([hn], jnp.pad(w_in[:, o[6]:o[7]], ((0, 0), (0, LANES - 3 * NSA_HEADS))))
    proj["nsz"] = matmul([hn], w_in[:, o[7]:o[8]])
    return proj


def _c_projections(hn, w_in):
    return dict(q=matmul([hn], w_in, col0=0, n=C_W), kv=matmul([hn], w_in, col0=C_W, n=2 * C_W),
                z=matmul([hn], w_in, col0=3 * C_W, n=C_W))


def kernel(x_prompt, x_sample, p_prompt, p_sample, cache_sb_kv, cache_nsa_kv, state_nsa_win_kv, cache_diff_kv,
           page_table, norm_g, w_in_ab, nsa_cmp_wk, nsa_cmp_wv, w_out_ab, w_in_c, diff_lq1, diff_lk1, diff_lq2,
           diff_lk2, diff_head_g, w_out_c, ple_norm_g, w_ple_gate, w_ple_proj, final_norm_g):
    bp, tp, d = x_prompt.shape
    bs, ts, _ = x_sample.shape
    depth = norm_g.shape[0]
    n_phys = cache_sb_kv.shape[1]

    def run(x, p_emb, sample):
        b, t, _ = x.shape
        h = x.reshape(b * t, d)
        sb_rows, nsa_rows, win_rows, diff_rows = [], [], [], []
        for i in range(depth):
            j = i // 2
            hn = rmsnorm(h, norm_g[i], BF16)
            if i % 2 == 0:
                proj = {name: y.reshape(b, t, -1) for name, y in _ab_projections(hn, w_in_ab[j]).items()}
                win_new = proj["win"].reshape(b, t, 2, NSA_KV_HEADS, HEAD_DIM)
                if sample:
                    sb_mixed = sb_decode(proj["sbq"], proj["sbkv"],
                                         cache_sb_kv.reshape(-1, PAGE_SIZE, 2 * SB_HEADS, HEAD_DIM), j * n_phys,
                                         page_table, proj["sbz"])
                    nsa_cache = cache_nsa_kv.reshape(-1, PAGE_SIZE, 4 * NSA_KV_HEADS, HEAD_DIM)
                    cmp_even, cmp_odd = nsa_compress_pages(nsa_cache, j * n_phys, page_table, nsa_cmp_wk[j],
                                                           nsa_cmp_wv[j])
                    win_rows_in = state_nsa_win_kv[j].reshape(b, -1, 2 * NSA_KV_HEADS, HEAD_DIM)
                    ns_mixed = nsa_decode(proj["nsq"], cmp_even, cmp_odd, nsa_cache, j * n_phys, page_table,
                                          proj["nskv"], win_rows_in, proj["win"], proj["logits"], proj["nsz"])
                    win_all = jnp.concatenate([state_nsa_win_kv[j], win_new], axis=1)
                else:
                    sb_mixed = sb_prompt(proj["sbq"], proj["sbkv"], proj["sbz"])
                    ns_mixed = nsa_prompt(proj["nsq"], proj["nskv"], proj["win"], proj["logits"], proj["nsz"],
                                          nsa_cmp_wk[j], nsa_cmp_wv[j])
                    win_all = win_new
                h = matmul([sb_mixed.reshape(b * t, -1), ns_mixed.reshape(b * t, -1)], w_out_ab[j], residual=h)
                sb_rows.append(proj["sbkv"].reshape(b, t, 2, SB_HEADS, HEAD_DIM))
                nsa_rows.append(proj["nskv"].reshape(b, t, 4, NSA_KV_HEADS, HEAD_DIM))
                keep = min(WINDOW, win_all.shape[1])
                win_rows.append(win_all[:, win_all.shape[1] - keep:])
            else:
                lambda_init = 0.8 - 0.6 * math.exp(-0.3 * i)
                proj = {name: y.reshape(b, t, -1) for name, y in _c_projections(hn, w_in_c[j]).items()}
                lam_args = (diff_lq1[j], diff_lk1[j], diff_lq2[j], diff_lk2[j], diff_head_g[j], lambda_init)
                if sample:
                    mixed = diff_decode(proj["q"], proj["kv"].reshape(b, t, 2 * DIFF_HEADS, DIFF_VDIM),
                                        cache_diff_kv.reshape(-1, PAGE_SIZE, 2 * DIFF_HEADS, DIFF_VDIM), j * n_phys,
                                        page_table, proj["z"], *lam_args)
                else:
                    mixed = diff_prompt(proj["q"], proj["kv"], proj["z"], *lam_args)
                h = matmul([mixed.reshape(b * t, -1)], w_out_c[j], residual=h)
                diff_rows.append(proj["kv"].reshape(b, t, 2, DIFF_HEADS, DIFF_VDIM))
            hn2 = rmsnorm(h, ple_norm_g[i], BF16)
            h = ple(hn2, w_ple_gate, p_emb[i].reshape(b * t, -1).astype(BF16), w_ple_proj, h, i)
        y = rmsnorm(h, final_norm_g, F32).reshape(b, t, d)
        return y, jnp.stack(sb_rows), jnp.stack(nsa_rows), jnp.stack(win_rows), jnp.stack(diff_rows)

    y_p, sb_p, nsa_p, win_p, diff_p = run(x_prompt, p_prompt, False)
    y_s, sb_s, nsa_s, win_s, diff_s = run(x_sample, p_sample, True)
    return (y_p, y_s, sb_p, sb_s, nsa_p, nsa_s, win_p, win_s, diff_p, diff_s)
```
